```python
import math, functools
import jax, jax.numpy as jnp
from jax import lax
import numpy as np

D_MODEL = 2048
BATCH = 2
SEQ = 8192
DEPTH = 1
DEC_BATCH = 32
DEC_SEQ = 8
PAST_LEN = 16384
PAGE_SIZE = 128

N_MEM = 256
SB_HEADS = 8
SB_HEAD_DIM = 128
SB_WIDTH = SB_HEADS * SB_HEAD_DIM
SB_QBLOCK = 128
SB_LOGIT_BIAS = -8.0
DN_HEADS = 8
DN_KDIM = 128
DN_VDIM = 128
DN_KW = DN_HEADS * DN_KDIM
DN_VW = DN_HEADS * DN_VDIM
DN_CONV = 4
DN_CONV_DIM = 2 * DN_KW + DN_VW
DN_CHUNK = 64
X_HEADS = 4
X_HEAD_DIM = 256
X_WIDTH = X_HEADS * X_HEAD_DIM
N_GROUPS = 4
EXPERTS_PER_GROUP = 8
N_EXPERTS = N_GROUPS * EXPERTS_PER_GROUP
TOP_K = 2
D_EXPERT = 1024
EXPERT_BLOCK = 128
RMS_EPS = 1e-6

IN_SIZES = (SB_WIDTH, SB_WIDTH, SB_WIDTH, DN_KW, DN_KW, DN_VW, DN_VW, DN_HEADS, DN_HEADS,
            X_WIDTH, D_MODEL, D_MODEL, D_MODEL)
IN_COLS = sum(IN_SIZES)

kernel_name = 'stickbreak_gdn_memxattn_hiermoe_step'


def _rmsnorm(x, w):
    xf = x.astype(jnp.float32)
    y = xf * lax.rsqrt(jnp.mean(xf * xf, axis=-1, keepdims=True) + RMS_EPS)
    return (y * w.astype(jnp.float32)).astype(x.dtype)


def _l2norm(x):
    return x * lax.rsqrt(jnp.sum(x * x, axis=-1, keepdims=True) + 1e-6)


def _split_cols(a, sizes):
    outs, start = [], 0
    for s in sizes:
        outs.append(a[..., start:start + s])
        start += s
    return outs


def _sb_prompt(q, k, v, bias):
    B, S, H, d = q.shape
    scale = d ** -0.5
    kf = k.astype(jnp.float32)
    vf = v.astype(jnp.float32)
    bf = bias.astype(jnp.float32)[None, :, None, None]
    key_pos = jnp.arange(S)

    def block(i):
        qb = lax.dynamic_slice_in_dim(q, i * SB_QBLOCK, SB_QBLOCK, axis=1).astype(jnp.float32)
        z = jnp.einsum('bqhd,bkhd->bhqk', qb, kf) * scale + bf
        q_pos = i * SB_QBLOCK + jnp.arange(SB_QBLOCK)
        strict = key_pos[None, :] < q_pos[:, None]
        log_1m = jnp.where(strict, jax.nn.log_sigmoid(-z), 0.0)
        after = lax.cumsum(log_1m, axis=3, reverse=True) - log_1m
        w = jnp.where(strict, jnp.exp(jax.nn.log_sigmoid(z) + after), 0.0)
        return jnp.einsum('bhqk,bkhd->bqhd', w, vf)

    out = lax.map(block, jnp.arange(S // SB_QBLOCK))
    return jnp.moveaxis(out, 0, 1).reshape(B, S, H, d).astype(q.dtype)


def _sb_sample(q, k, v, bias, cache_k, cache_v, page_table, layer):
    DB, Q, H, d = q.shape
    scale = d ** -0.5
    qf = q.astype(jnp.float32)
    bf = bias.astype(jnp.float32)[None, :, None, None]
    z = jnp.einsum('bqhd,bkhd->bhqk', qf, k.astype(jnp.float32)) * scale + bf
    strict = jnp.tril(jnp.ones((Q, Q), bool), -1)
    log_1m = jnp.where(strict, jax.nn.log_sigmoid(-z), 0.0)
    after = lax.cumsum(log_1m, axis=3, reverse=True) - log_1m
    w = jnp.where(strict, jnp.exp(jax.nn.log_sigmoid(z) + after), 0.0)
    out0 = jnp.einsum('bhqk,bkhd->bqhd', w, v.astype(jnp.float32))
    acc0 = jnp.sum(log_1m, axis=-1)

    def page_step(carry, p):
        acc, out = carry
        phys = page_table[:, p]
        kp = cache_k[layer, phys].astype(jnp.float32)
        vp = cache_v[layer, phys].astype(jnp.float32)
        zp = jnp.einsum('bqhd,bkhd->bhqk', qf, kp) * scale + bf
        l1 = jax.nn.log_sigmoid(-zp)
        wp = jnp.exp(jax.nn.log_sigmoid(zp) + acc[..., None]
                     + lax.cumsum(l1, axis=3, reverse=True) - l1)
        out = out + jnp.einsum('bhqk,bkhd->bqhd', wp, vp)
        return (acc + jnp.sum(l1, axis=-1), out), None

    n_pages = page_table.shape[1]
    (_, out), _ = lax.scan(page_step, (acc0, out0), jnp.arange(n_pages - 1, -1, -1))
    return out.astype(q.dtype)


def _gated_delta_chunked(q, k, v, g, beta, state0):
    B, L, H, DK = q.shape
    DV = v.shape[-1]
    C = math.gcd(L, DN_CHUNK)
    N = L // C

    def chunks(a):
        a = jnp.moveaxis(a, 2, 1)
        return a.reshape(B, H, N, C, *a.shape[3:])

    qc, kc, vc, gc, bc = chunks(q), chunks(k), chunks(v), chunks(g), chunks(beta)
    gcum = jnp.cumsum(gc, axis=-1)
    causal = jnp.tril(jnp.ones((C, C), bool))
    decay = jnp.exp(jnp.where(causal, gcum[..., :, None] - gcum[..., None, :], -jnp.inf))
    kb = kc * bc[..., None]
    strict = causal & ~jnp.eye(C, dtype=bool)
    lower = jnp.where(strict, jnp.einsum('bhncd,bhnmd->bhncm', kb, kc) * decay, 0.0)
    tri = lower + jnp.eye(C, dtype=lower.dtype)
    rhs = jnp.concatenate([vc * bc[..., None], kb * jnp.exp(gcum)[..., None]], axis=-1)
    sol = lax.linalg.triangular_solve(tri, rhs, left_side=True, lower=True, unit_diagonal=True)
    u, w = sol[..., :DV], sol[..., DV:]
    qk = jnp.where(causal, jnp.einsum('bhncd,bhnmd->bhncm', qc, kc) * decay, 0.0)

    def step(S, inp):
        q_i, k_i, u_i, w_i, g_i, qk_i = inp
        v_new = u_i - jnp.einsum('bhcd,bhde->bhce', w_i, S)
        o = (jnp.einsum('bhcd,bhde->bhce', q_i * jnp.exp(g_i)[..., None], S)
             + jnp.einsum('bhcm,bhme->bhce', qk_i, v_new))
        g_last = g_i[..., -1:]
        S = (S * jnp.exp(g_last)[..., None]
             + jnp.einsum('bhcd,bhce->bhde', k_i * jnp.exp(g_last - g_i)[..., None], v_new))
        return S, o

    xs = tuple(jnp.moveaxis(a, 2, 0) for a in (qc, kc, u, w, gcum, qk))
    S, o = lax.scan(step, state0, xs)
    o = jnp.moveaxis(o, 0, 2).reshape(B, H, L, DV)
    return jnp.moveaxis(o, 1, 2), S


def _deltanet_branch(q, k, v, z, a, b, conv_prev, state0, conv_w, a_log, dt_bias, norm_w):
    B, L, _ = q.shape
    qkv = jnp.concatenate([q, k, v], axis=-1)
    seq = jnp.concatenate([conv_prev.astype(qkv.dtype), qkv], axis=1)
    conv = seq[:, 0:L] * conv_w[0]
    for i in range(1, DN_CONV):
        conv = conv + seq[:, i:i + L] * conv_w[i]
    conv = jax.nn.silu(conv.astype(jnp.float32))
    new_conv = seq[:, L:]
    qc, kc, vc = _split_cols(conv, (DN_KW, DN_KW, DN_VW))
    qc = _l2norm(qc.reshape(B, L, DN_HEADS, DN_KDIM)) * (DN_KDIM ** -0.5)
    kc = _l2norm(kc.reshape(B, L, DN_HEADS, DN_KDIM))
    vc = vc.reshape(B, L, DN_HEADS, DN_VDIM)
    beta = jax.nn.sigmoid(b.astype(jnp.float32))
    g = -jnp.exp(a_log.astype(jnp.float32)) * jax.nn.softplus(a.astype(jnp.float32) + dt_bias.astype(jnp.float32))
    o, S = _gated_delta_chunked(qc, kc, vc, g, beta, state0.astype(jnp.float32))
    o = _rmsnorm(o, norm_w) * jax.nn.silu(z.astype(jnp.float32).reshape(B, L, DN_HEADS, DN_VDIM))
    return o.reshape(B, L, DN_VW).astype(q.dtype), S, new_conv


def _mem_kv(mem, norm_w, w_k, w_v):
    B, M, _ = mem.shape
    mn = _rmsnorm(mem, norm_w)
    return ((mn @ w_k).reshape(B, M, X_HEADS, X_HEAD_DIM),
            (mn @ w_v).reshape(B, M, X_HEADS, X_HEAD_DIM))


def _cross_attend(q, mem_k, mem_v):
    d = q.shape[-1]
    s = jnp.einsum('blhd,bmhd->bhlm', q.astype(jnp.float32), mem_k.astype(jnp.float32)) * (d ** -0.5)
    p = jax.nn.softmax(s, axis=-1)
    return jnp.einsum('bhlm,bmhd->blhd', p, mem_v.astype(jnp.float32)).astype(q.dtype)


def _expert_dispatch(x, e_ids, gates, w1, w3, w2):
    T, D = x.shape
    A = e_ids.shape[0]
    tok = jnp.arange(A) // TOP_K
    order = jnp.argsort(e_ids)
    se, st, sg = e_ids[order], tok[order], gates[order]
    counts = jnp.zeros((N_EXPERTS,), jnp.int32).at[e_ids].add(1)
    padded = ((counts + EXPERT_BLOCK - 1) // EXPERT_BLOCK) * EXPERT_BLOCK
    start = jnp.cumsum(counts) - counts
    pend = jnp.cumsum(padded)
    pstart = pend - padded
    dest = pstart[se] + jnp.arange(A) - start[se]
    n_blk = (A + N_EXPERTS * (EXPERT_BLOCK - 1) + EXPERT_BLOCK - 1) // EXPERT_BLOCK
    P = n_blk * EXPERT_BLOCK
    slot_tok = jnp.zeros((P,), jnp.int32).at[dest].set(st)
    slot_gate = jnp.zeros((P,), x.dtype).at[dest].set(sg.astype(x.dtype))
    blk_exp = jnp.minimum(jnp.searchsorted(pend, jnp.arange(n_blk) * EXPERT_BLOCK, side='right'),
                          N_EXPERTS - 1)
    xb = x[slot_tok].reshape(n_blk, EXPERT_BLOCK, D)

    def expert_block(args):
        xblk, e = args
        h = jax.nn.silu(xblk @ w1[e]) * (xblk @ w3[e])
        return h @ w2[e]

    yb = lax.map(expert_block, (xb, blk_exp)).reshape(P, D)
    return jnp.zeros_like(x).at[slot_tok].add(yb * slot_gate[:, None])


def _hier_moe(x, w_rg, b_rg, w_re, b_re, w1, w3, w2):
    T = x.shape[0]
    g_logits = (x @ w_rg).astype(jnp.float32) + b_rg.astype(jnp.float32)
    g_p, g_idx = lax.top_k(jax.nn.softmax(g_logits, axis=-1), 1)
    e_logits = ((x @ w_re).astype(jnp.float32) + b_re.astype(jnp.float32)).reshape(
        T, N_GROUPS, EXPERTS_PER_GROUP)
    e_in_group = e_logits[jnp.arange(T), g_idx[:, 0]]
    e_p, e_idx = lax.top_k(jax.nn.softmax(e_in_group, axis=-1), TOP_K)
    gate = g_p * e_p / jnp.sum(e_p, axis=-1, keepdims=True)
    expert = g_idx * EXPERTS_PER_GROUP + e_idx
    return _expert_dispatch(x, expert.reshape(-1), gate.reshape(-1), w1, w3, w2)


def _layer(x, sb_attend, mem_k, mem_v, conv_prev, dn_state0,
           norm_mix_w, w_in, sb_bias, dn_conv_w, dn_a_log, dn_dt_bias, dn_norm_w,
           w_sb_o, w_dn_o, w_x_o, w_out, norm_ffn_w,
           w_router_g, b_router_g, w_router_e, b_router_e, w_e1, w_e3, w_e2):
    B, L, D = x.shape
    xn = _rmsnorm(x, norm_mix_w)
    (q_sb, k_sb, v_sb, q_dn, k_dn, v_dn, z_dn, a_dn, b_dn, q_x,
     g_sb, g_dn, g_x) = _split_cols(xn @ w_in, IN_SIZES)
    q_sb = q_sb.reshape(B, L, SB_HEADS, SB_HEAD_DIM)
    k_sb = k_sb.reshape(B, L, SB_HEADS, SB_HEAD_DIM)
    v_sb = v_sb.reshape(B, L, SB_HEADS, SB_HEAD_DIM)
    o_sb = sb_attend(q_sb, k_sb, v_sb, sb_bias).reshape(B, L, SB_WIDTH)
    o_dn, dn_state, dn_conv = _deltanet_branch(q_dn, k_dn, v_dn, z_dn, a_dn, b_dn, conv_prev, dn_state0,
                                               dn_conv_w, dn_a_log, dn_dt_bias, dn_norm_w)
    o_x = _cross_attend(q_x.reshape(B, L, X_HEADS, X_HEAD_DIM), mem_k, mem_v).reshape(B, L, X_WIDTH)
    merged = (jax.nn.sigmoid(g_sb) * (o_sb @ w_sb_o)
              + jax.nn.sigmoid(g_dn) * (o_dn @ w_dn_o)
              + jax.nn.sigmoid(g_x) * (o_x @ w_x_o))
    h = x + merged @ w_out
    hn = _rmsnorm(h, norm_ffn_w).reshape(B * L, D)
    h = h + _hier_moe(hn, w_router_g, b_router_g, w_router_e, b_router_e, w_e1, w_e3, w_e2).reshape(B, L, D)
    return h, k_sb, v_sb, dn_state, dn_conv


def setup_inputs(seed: int = 0) -> dict:
    key = jax.random.key(seed)
    k = list(jax.random.split(key, 33))
    f32 = jnp.float32
    n_pages = PAST_LEN // PAGE_SIZE
    n_pool = (5 * DEC_BATCH * n_pages + 3) // 4

    def nrm(i, shape, scale):
        return jax.random.normal(k[i], shape, f32) * scale

    def gain(i, shape):
        return 1.0 + 0.02 * jax.random.normal(k[i], shape, f32)

    page_table = jax.random.permutation(k[8], n_pool)[:DEC_BATCH * n_pages].reshape(
        DEC_BATCH, n_pages).astype(jnp.int32)
    dn_a_log = jnp.log(jax.random.uniform(k[13], (DEPTH, DN_HEADS), f32, 1.0, 16.0))
    dt = jnp.exp(jax.random.uniform(k[14], (DEPTH, DN_HEADS), f32, math.log(1e-3), math.log(1e-1)))
    dn_dt_bias = dt + jnp.log(-jnp.expm1(-dt))
    return {
        'x_prompt': nrm(0, (BATCH, SEQ, D_MODEL), 1.0),
        'x_sample': nrm(1, (DEC_BATCH, DEC_SEQ, D_MODEL), 1.0),
        'cache_sb_k': nrm(2, (DEPTH, n_pool, PAGE_SIZE, SB_HEADS, SB_HEAD_DIM), 1.0),
        'cache_sb_v': nrm(3, (DEPTH, n_pool, PAGE_SIZE, SB_HEADS, SB_HEAD_DIM), 1.0),
        'cache_mem_k': nrm(4, (DEPTH, DEC_BATCH, N_MEM, X_HEADS, X_HEAD_DIM), 1.0),
        'cache_mem_v': nrm(5, (DEPTH, DEC_BATCH, N_MEM, X_HEADS, X_HEAD_DIM), 1.0),
        'state_dn': nrm(6, (DEPTH, DEC_BATCH, DN_HEADS, DN_KDIM, DN_VDIM), 0.5),
        'state_dn_conv': nrm(7, (DEPTH, DEC_BATCH, DN_CONV - 1, DN_CONV_DIM), 1.0),
        'page_table': page_table,
        'mem_prompt': nrm(9, (BATCH, N_MEM, D_MODEL), 1.0),
        'norm_mix_w': gain(10, (DEPTH, D_MODEL)),
        'w_in': nrm(11, (DEPTH, D_MODEL, IN_COLS), D_MODEL ** -0.5),
        'sb_bias': SB_LOGIT_BIAS + nrm(32, (DEPTH, SB_HEADS), 0.5),
        'dn_conv_w': nrm(12, (DEPTH, DN_CONV, DN_CONV_DIM), DN_CONV ** -0.5),
        'dn_a_log': dn_a_log,
        'dn_dt_bias': dn_dt_bias,
        'dn_norm_w': gain(15, (DEPTH, DN_VDIM)),
        'mem_norm_w': gain(16, (DEPTH, D_MODEL)),
        'w_mem_k': nrm(17, (DEPTH, D_MODEL, X_WIDTH), D_MODEL ** -0.5),
        'w_mem_v': nrm(18, (DEPTH, D_MODEL, X_WIDTH), D_MODEL ** -0.5),
        'w_sb_o': nrm(19, (DEPTH, SB_WIDTH, D_MODEL), SB_WIDTH ** -0.5),
        'w_dn_o': nrm(20, (DEPTH, DN_VW, D_MODEL), DN_VW ** -0.5),
        'w_x_o': nrm(21, (DEPTH, X_WIDTH, D_MODEL), X_WIDTH ** -0.5),
        'w_out': nrm(22, (DEPTH, D_MODEL, D_MODEL), D_MODEL ** -0.5),
        'norm_ffn_w': gain(23, (DEPTH, D_MODEL)),
        'w_router_g': nrm(24, (DEPTH, D_MODEL, N_GROUPS), D_MODEL ** -0.5),
        'b_router_g': nrm(25, (DEPTH, N_GROUPS), 0.01),
        'w_router_e': nrm(26, (DEPTH, D_MODEL, N_EXPERTS), D_MODEL ** -0.5),
        'b_router_e': nrm(27, (DEPTH, N_EXPERTS), 0.01),
        'w_e1': nrm(28, (DEPTH, N_EXPERTS, D_MODEL, D_EXPERT), D_MODEL ** -0.5),
        'w_e3': nrm(29, (DEPTH, N_EXPERTS, D_MODEL, D_EXPERT), D_MODEL ** -0.5),
        'w_e2': nrm(30, (DEPTH, N_EXPERTS, D_EXPERT, D_MODEL), D_EXPERT ** -0.5),
        'norm_final_w': gain(31, (D_MODEL,)),
    }


def reference(x_prompt, x_sample, cache_sb_k, cache_sb_v, cache_mem_k, cache_mem_v,
              state_dn, state_dn_conv, page_table, mem_prompt,
              norm_mix_w, w_in, sb_bias, dn_conv_w, dn_a_log, dn_dt_bias, dn_norm_w,
              mem_norm_w, w_mem_k, w_mem_v, w_sb_o, w_dn_o, w_x_o, w_out,
              norm_ffn_w, w_router_g, b_router_g, w_router_e, b_router_e,
              w_e1, w_e3, w_e2, norm_final_w):
    hp, hs = x_prompt, x_sample
    bp = x_prompt.shape[0]
    sb_kp, sb_vp, mem_kp, mem_vp, dn_sp, dn_cp = [], [], [], [], [], []
    sb_ks, sb_vs, dn_ss, dn_cs = [], [], [], []
    for l in range(DEPTH):
        lw = (norm_mix_w[l], w_in[l], sb_bias[l], dn_conv_w[l], dn_a_log[l], dn_dt_bias[l], dn_norm_w[l],
              w_sb_o[l], w_dn_o[l], w_x_o[l], w_out[l], norm_ffn_w[l],
              w_router_g[l], b_router_g[l], w_router_e[l], b_router_e[l],
              w_e1[l], w_e3[l], w_e2[l])
        mk, mv = _mem_kv(mem_prompt, mem_norm_w[l], w_mem_k[l], w_mem_v[l])
        conv0 = jnp.zeros((bp, DN_CONV - 1, DN_CONV_DIM), x_prompt.dtype)
        s0 = jnp.zeros((bp, DN_HEADS, DN_KDIM, DN_VDIM), jnp.float32)
        hp, k_new, v_new, s_new, c_new = _layer(hp, _sb_prompt, mk, mv, conv0, s0, *lw)
        sb_kp.append(k_new)
        sb_vp.append(v_new)
        mem_kp.append(mk)
        mem_vp.append(mv)
        dn_sp.append(s_new)
        dn_cp.append(c_new)
        sb_attend = functools.partial(_sb_sample, cache_k=cache_sb_k, cache_v=cache_sb_v,
                                      page_table=page_table, layer=l)
        hs, k_new, v_new, s_new, c_new = _layer(hs, sb_attend, cache_mem_k[l], cache_mem_v[l],
                                                state_dn_conv[l], state_dn[l], *lw)
        sb_ks.append(k_new)
        sb_vs.append(v_new)
        dn_ss.append(s_new)
        dn_cs.append(c_new)
    y_prompt = _rmsnorm(hp, norm_final_w)
    y_sample = _rmsnorm(hs, norm_final_w)
    sb_k_prompt = jnp.stack(sb_kp)
    sb_v_prompt = jnp.stack(sb_vp)
    mem_k_prompt = jnp.stack(mem_kp)
    mem_v_prompt = jnp.stack(mem_vp)
    dn_state_prompt = jnp.stack(dn_sp)
    dn_conv_prompt = jnp.stack(dn_cp)
    sb_k_sample = jnp.stack(sb_ks)
    sb_v_sample = jnp.stack(sb_vs)
    dn_state_sample = jnp.stack(dn_ss)
    dn_conv_sample = jnp.stack(dn_cs)
    return (y_prompt, y_sample, sb_k_prompt, sb_v_prompt, mem_k_prompt, mem_v_prompt,
            dn_state_prompt, dn_conv_prompt, sb_k_sample, sb_v_sample, dn_state_sample, dn_conv_sample)
```

```python
import functools

import jax
import jax.numpy as jnp
from jax import lax
from jax.experimental import pallas as pl
from jax.experimental.pallas import tpu as pltpu

F32, BF16, I32 = jnp.float32, jnp.bfloat16, jnp.int32

RMS_EPS = 1e-6
L2_EPS = 1e-6
SB_HEADS, SB_HEAD_DIM = 8, 128
DN_HEADS, DN_KDIM, DN_VDIM, DN_CONV = 8, 128, 128, 4
X_HEADS, X_HEAD_DIM = 4, 256
N_GROUPS, EXPERTS_PER_GROUP, TOP_K = 4, 8, 2
N_EXPERTS = N_GROUPS * EXPERTS_PER_GROUP
PAGE_SIZE = 128

LANES = 128
MIB = 1024 * 1024

SB_W = SB_HEADS * SB_HEAD_DIM
DN_W = DN_HEADS * DN_KDIM
X_W = X_HEADS * X_HEAD_DIM
COL_QSB, COL_KSB, COL_VSB = 0, SB_W, 2 * SB_W
COL_QDN = 3 * SB_W
COL_KDN, COL_VDN, COL_ZDN = COL_QDN + DN_W, COL_QDN + 2 * DN_W, COL_QDN + 3 * DN_W
COL_AB = COL_ZDN + DN_W
COL_QX = COL_AB
NT = (((1,), (1,)), ((), ()))
TN = (((0,), (0,)), ((), ()))


def _cparams(sem, vmem_mib):
    return pltpu.CompilerParams(dimension_semantics=sem, vmem_limit_bytes=int(vmem_mib * MIB))


def _dot(a, b):
    return jnp.dot(a.astype(BF16), b.astype(BF16), preferred_element_type=F32)


def _dot_g(a, b, dims):
    return lax.dot_general(a.astype(BF16), b.astype(BF16), dims, preferred_element_type=F32)


def _split2(a):
    hi = a.astype(BF16)
    lo = (a - hi.astype(F32)).astype(BF16)
    return hi, lo


def _dot3(a, b):
    ah, al = _split2(a)
    bh, bl = _split2(b)
    d = functools.partial(jnp.dot, preferred_element_type=F32)
    return d(ah, bh) + (d(ah, bl) + d(al, bh))


def _softplus(z):
    return jnp.maximum(z, 0.0) + jnp.log(1.0 + jnp.exp(-jnp.abs(z)))


def _sigmoid(z):
    return 1.0 / (1.0 + jnp.exp(-z))


def _silu(z):
    return z * _sigmoid(z)


def _iota2(shape):
    return lax.broadcasted_iota(I32, shape, 0), lax.broadcasted_iota(I32, shape, 1)


def _div_pow2(x, p):
    assert p & (p - 1) == 0, p
    return lax.shift_right_logical(x, p.bit_length() - 1)


def _proj_kernel(*refs, na, has_extra):
    if has_extra:
        x_ref, nw_ref, wa_ref, wb_ref, we_ref, o_ref, e_ref, xn_ref = refs
    else:
        x_ref, nw_ref, wa_ref, wb_ref, o_ref, xn_ref = refs
    j = pl.program_id(1)

    @pl.when(j == 0)
    def _():
        x = x_ref[...]
        ms = jnp.mean(x * x, axis=-1, keepdims=True)
        xn = (x * lax.rsqrt(ms + RMS_EPS) * nw_ref[...]).astype(BF16)
        xn_ref[...] = xn
        if has_extra:
            e_ref[...] = jnp.dot(xn, we_ref[...].astype(BF16), preferred_element_type=F32)

    @pl.when(j < na)
    def _():
        o_ref[...] = jnp.dot(xn_ref[...], wa_ref[...].astype(BF16), preferred_element_type=F32)

    @pl.when(j >= na)
    def _():
        o_ref[...] = jnp.dot(xn_ref[...], wb_ref[...].astype(BF16), preferred_element_type=F32)


def _proj(x, nw, wa, na_cols, wb, w_extra=None, *, tm=1024, tn=512):
    T, D = x.shape
    tm = min(tm, T)
    na, nb = na_cols // tn, wb.shape[1] // tn
    has_extra = w_extra is not None
    in_specs = [
        pl.BlockSpec((tm, D), lambda i, j: (i, 0)),
        pl.BlockSpec((1, D), lambda i, j: (0, 0)),
        pl.BlockSpec((D, tn), lambda i, j: (0, jnp.minimum(j, na - 1))),
        pl.BlockSpec((D, tn), lambda i, j: (0, jnp.maximum(j - na, 0))),
    ]
    out_shape = [jax.ShapeDtypeStruct((T, (na + nb) * tn), F32)]
    out_specs = [pl.BlockSpec((tm, tn), lambda i, j: (i, j))]
    args = [x, nw.reshape(1, D), wa, wb]
    if has_extra:
        in_specs.append(pl.BlockSpec((D, LANES), lambda i, j: (0, 0)))
        out_shape.append(jax.ShapeDtypeStruct((T, LANES), F32))
        out_specs.append(pl.BlockSpec((tm, LANES), lambda i, j: (i, 0)))
        args.append(w_extra)
    vmem = (2 * tm * D * 4 + tm * D * 2 + 4 * D * tn * 4 + 2 * tm * tn * 4 + 4 * MIB) / MIB + 6
    res = pl.pallas_call(
        functools.partial(_proj_kernel, na=na, has_extra=has_extra),
        grid=(T // tm, na + nb),
        in_specs=in_specs, out_specs=out_specs, out_shape=out_shape,
        scratch_shapes=[pltpu.VMEM((tm, D), BF16)],
        compiler_params=_cparams(("arbitrary", "arbitrary"), vmem),
        name="proj",
    )(*args)
    return res if has_extra else res[0]


def _sb_tile(z, mask, u_bf, acc, v_bf):
    sp = _softplus(z)
    ls = z - sp
    if mask is not None:
        sp = jnp.where(mask, sp, 0.0)
    hi, lo = _split2(sp)
    cum = (jnp.dot(hi, u_bf, preferred_element_type=F32)
           + jnp.dot(lo, u_bf, preferred_element_type=F32))
    tot = cum[:, 0:1] + sp[:, 0:1]
    w = jnp.exp(ls - cum - acc)
    if mask is not None:
        w = jnp.where(mask, w, 0.0)
    pv = jnp.dot(w.astype(BF16), v_bf, preferred_element_type=F32)
    return pv, tot


def _later_key_matrix(n):
    r, c = _iota2((n, n))
    return jnp.where(r > c, 1.0, 0.0).astype(BF16)


def _sb_prompt_kernel(bias_ref, q_ref, k_ref, v_ref, o_ref, kb_ref, vb_ref, *, tq, scale):
    h, i = pl.program_id(1), pl.program_id(2)

    @pl.when(i == 0)
    def _():
        kb_ref[...] = k_ref[...].astype(BF16)
        vb_ref[...] = v_ref[...].astype(BF16)

    bias = bias_ref[h]
    q = (q_ref[...] * scale).astype(BF16)
    u = _later_key_matrix(tq)
    r, c = _iota2((tq, tq))
    strict = c < r

    def kv(j):
        sl = pl.ds(pl.multiple_of(j * tq, tq), tq)
        return kb_ref[sl, :], vb_ref[sl, :]

    kd, vd = kv(i)
    z = lax.dot_general(q, kd, NT, preferred_element_type=F32) + bias
    o0, acc0 = _sb_tile(z, strict, u, jnp.zeros((tq, 1), F32), vd)

    def body(t, carry):
        o, acc = carry
        kj, vj = kv(i - 1 - t)
        zj = lax.dot_general(q, kj, NT, preferred_element_type=F32) + bias
        pv, tot = _sb_tile(zj, None, u, acc, vj)
        return o + pv, acc + tot

    o, _ = lax.fori_loop(0, i, body, (o0, acc0))
    o_ref[...] = o.astype(o_ref.dtype)


def _sb_prompt(proj, bias, B, S, *, tq=256):
    H, d = SB_HEADS, SB_HEAD_DIM
    nq = S // tq
    return pl.pallas_call(
        functools.partial(_sb_prompt_kernel, tq=tq, scale=d ** -0.5),
        grid=(B, H, nq),
        in_specs=[
            pl.BlockSpec(memory_space=pltpu.SMEM),
            pl.BlockSpec((tq, d), lambda b, h, i: (b * nq + i, COL_QSB // d + h)),
            pl.BlockSpec((S, d), lambda b, h, i: (b, COL_KSB // d + h)),
            pl.BlockSpec((S, d), lambda b, h, i: (b, COL_VSB // d + h)),
        ],
        out_specs=pl.BlockSpec((tq, d), lambda b, h, i: (b * nq + i, h)),
        out_shape=jax.ShapeDtypeStruct((B * S, H * d), BF16),
        scratch_shapes=[pltpu.VMEM((S, d), BF16), pltpu.VMEM((S, d), BF16)],
        compiler_params=_cparams(("arbitrary",) * 3, 4 * S * d * 4 / MIB + 2 * S * d * 2 / MIB + 16),
        name="sb_prompt",
    )(bias, proj, proj, proj)


def _sb_sample_kernel(pt_ref, bias_ref, q_ref, kn_ref, vn_ref, *rest, G, scale):
    del pt_ref
    k_refs, v_refs = rest[:G], rest[G:2 * G]
    o_ref, qbd_ref, acc_ref, out_ref = rest[2 * G:]
    H, d, P = SB_HEADS, SB_HEAD_DIM, PAGE_SIZE
    Q, W = q_ref.shape
    HQ = H * Q
    g = pl.program_id(1)
    u = _later_key_matrix(P)

    def page(kf, vf, mask):
        z = lax.dot_general(qbd_ref[...], kf.astype(BF16), NT, preferred_element_type=F32) + bias_ref[...]
        pv, tot = _sb_tile(z, mask, u, acc_ref[...], vf.astype(BF16))
        for h in range(H):
            out_ref[h * Q:(h + 1) * Q, :] += pv[h * Q:(h + 1) * Q, h * d:(h + 1) * d]
        acc_ref[...] += tot

    @pl.when(g == 0)
    def _():
        r, c = _iota2((HQ, W))
        qrep = jnp.concatenate([q_ref[...] * scale] * H, axis=0)
        qbd_ref[...] = jnp.where(_div_pow2(r, Q) == _div_pow2(c, d), qrep, 0.0).astype(BF16)
        acc_ref[...] = jnp.zeros_like(acc_ref)
        out_ref[...] = jnp.zeros_like(out_ref)
        pad = jnp.zeros((P - Q, W), F32)
        rq, ck = _iota2((HQ, P))
        page(jnp.concatenate([kn_ref[...], pad], axis=0),
             jnp.concatenate([vn_ref[...], pad], axis=0), ck < jnp.bitwise_and(rq, Q - 1))

    for t in range(G):
        page(k_refs[t][0], v_refs[t][0], None)

    @pl.when(g == pl.num_programs(1) - 1)
    def _():
        for h in range(H):
            o_ref[:, h * d:(h + 1) * d] = out_ref[h * Q:(h + 1) * Q, :].astype(o_ref.dtype)


def _sb_sample(proj, bias, cache_k, cache_v, page_table, layer, DB, Q, *, G=8):
    H, d = SB_HEADS, SB_HEAD_DIM
    W = H * d
    depth, n_pool = cache_k.shape[:2]
    NP = page_table.shape[1]
    ck = cache_k.reshape(depth * n_pool, PAGE_SIZE, W)
    cv = cache_v.reshape(depth * n_pool, PAGE_SIZE, W)
    base = layer * n_pool
    bias_rows = jnp.broadcast_to(jnp.repeat(bias, Q)[:, None], (H * Q, PAGE_SIZE)).astype(F32)

    def page_spec(t):
        return pl.BlockSpec((1, PAGE_SIZE, W), lambda b, g, pt: (base + pt[b, NP - 1 - (g * G + t)], 0, 0))

    in_specs = [
        pl.BlockSpec((H * Q, PAGE_SIZE), lambda b, g, pt: (0, 0)),
        pl.BlockSpec((Q, W), lambda b, g, pt: (b, COL_QSB // W)),
        pl.BlockSpec((Q, W), lambda b, g, pt: (b, COL_KSB // W)),
        pl.BlockSpec((Q, W), lambda b, g, pt: (b, COL_VSB // W)),
    ] + [page_spec(t) for t in range(G)] * 2
    return pl.pallas_call(
        functools.partial(_sb_sample_kernel, G=G, scale=d ** -0.5),
        grid_spec=pltpu.PrefetchScalarGridSpec(
            num_scalar_prefetch=1, grid=(DB, NP // G),
            in_specs=in_specs,
            out_specs=pl.BlockSpec((Q, W), lambda b, g, pt: (b, 0)),
            scratch_shapes=[pltpu.VMEM((H * Q, W), BF16), pltpu.VMEM((H * Q, 1), F32),
                            pltpu.VMEM((H * Q, d), F32)]),
        out_shape=jax.ShapeDtypeStruct((DB * Q, W), BF16),
        compiler_params=_cparams(("arbitrary", "arbitrary"), 4 * G * PAGE_SIZE * W * 4 / MIB + 12),
        name="sb_sample",
    )(page_table, bias_rows, proj, proj, proj, *([ck] * G), *([cv] * G))


def _unit_lower_inverse(lm, n):
    r, c = _iota2((n, n))
    eye = jnp.where(r == c, 1.0, 0.0)
    l16 = jnp.where(_div_pow2(r, 16) == _div_pow2(c, 16), lm, 0.0)
    l2 = _dot3(l16, l16)
    l4 = _dot3(l2, l2)
    l8 = _dot3(l4, l4)
    x = eye - l16
    x = x + _dot3(x, l2)
    x = x + _dot3(x, l4)
    x = x + _dot3(x, l8)
    size = 32
    while size <= n:
        off = jnp.where(_div_pow2(r, size) == _div_pow2(c, size),
                        jnp.where(_div_pow2(r, size // 2) != _div_pow2(c, size // 2), lm, 0.0), 0.0)
        x = x - _dot3(_dot3(x, off), x)
        size *= 2
    return x


def _dn_kernel(q_ref, k_ref, v_ref, z_ref, ab_ref, cpq_ref, cpk_ref, cpv_ref, cwq_ref, cwk_ref, cwv_ref,
               s0_ref, alog_ref, dtb_ref, nw_ref, o_ref, s_out_ref, s_ref, prev_ref, seq_ref, *, TL, CP):
    h, i = pl.program_id(1), pl.program_id(2)
    K = DN_CONV - 1

    @pl.when(i == 0)
    def _():
        s_ref[...] = s0_ref[0, 0]
        prev_ref[...] = jnp.zeros_like(prev_ref)
        for n, cp in enumerate((cpq_ref, cpk_ref, cpv_ref)):
            prev_ref[n, 8 - K:8, :] = cp[0]

    def conv(n, x_ref, cw_ref):
        seq_ref[0:8, :] = prev_ref[n]
        seq_ref[8:8 + TL, :] = x_ref[...]
        cw = cw_ref[...]
        acc = seq_ref[8:8 + TL, :] * cw[K:K + 1, :]
        for s in range(1, DN_CONV):
            acc = acc + seq_ref[8 - s:8 - s + TL, :] * cw[K - s:K - s + 1, :]
        prev_ref[n] = seq_ref[TL:TL + 8, :]
        y = _silu(acc)
        if CP > TL:
            y = jnp.concatenate([y, jnp.zeros((CP - TL, y.shape[1]), F32)], axis=0)
        return y

    qc = conv(0, q_ref, cwq_ref)
    kc = conv(1, k_ref, cwk_ref)
    vc = conv(2, v_ref, cwv_ref)
    qn = qc * lax.rsqrt(jnp.sum(qc * qc, axis=-1, keepdims=True) + L2_EPS) * (DN_KDIM ** -0.5)
    kn = kc * lax.rsqrt(jnp.sum(kc * kc, axis=-1, keepdims=True) + L2_EPS)

    ab = ab_ref[...]
    lane = lax.broadcasted_iota(I32, ab.shape, 1)
    a = jnp.sum(jnp.where(lane == h, ab, 0.0), axis=-1, keepdims=True)
    bb = jnp.sum(jnp.where(lane == DN_HEADS + h, ab, 0.0), axis=-1, keepdims=True)
    beta = _sigmoid(bb)
    gl = -jnp.exp(alog_ref[0][:, 0:1]) * _softplus(a + dtb_ref[0][:, 0:1])
    if CP > TL:
        zpad = jnp.zeros((CP - TL, 1), F32)
        beta = jnp.concatenate([beta, zpad], axis=0)
        gl = jnp.concatenate([gl, zpad], axis=0)

    r, c = _iota2((CP, CP))
    tri = jnp.where(r >= c, 1.0, 0.0).astype(BF16)
    g1 = jnp.broadcast_to(gl, (CP, LANES))
    h1 = g1.astype(BF16)
    r1 = g1 - h1.astype(F32)
    h2 = r1.astype(BF16)
    h3 = (r1 - h2.astype(F32)).astype(BF16)
    d = functools.partial(jnp.dot, preferred_element_type=F32)
    gcum_b = d(tri, h1) + (d(tri, h2) + d(tri, h3))
    gc = gcum_b[:, 0:1]
    gr = jnp.transpose(gcum_b)[0:1, :]
    causal = r >= c
    decay = jnp.where(causal, jnp.exp(jnp.where(causal, gc - gr, 0.0)), 0.0)
    kb = kn * beta
    lm = jnp.where(r > c, _dot_g(kb, kn, NT) * decay, 0.0)
    qk = _dot_g(qn, kn, NT) * decay
    x = _unit_lower_inverse(lm, CP)
    eg = jnp.exp(gc)
    u = _dot3(x, vc * beta)
    w = _dot3(x, kb * eg)
    s = s_ref[...]
    v_new = u - _dot(w, s)
    o = _dot(qn * eg, s) + _dot(qk, v_new)
    g_last = gc[CP - 1:CP, :]
    s_new = s * jnp.exp(g_last) + _dot_g(kn * jnp.exp(g_last - gc), v_new, TN)
    s_ref[...] = s_new

    o = o[:TL]
    o = o * lax.rsqrt(jnp.mean(o * o, axis=-1, keepdims=True) + RMS_EPS) * nw_ref[...]
    o_ref[...] = (o * _silu(z_ref[...])).astype(o_ref.dtype)

    @pl.when(i == pl.num_programs(2) - 1)
    def _():
        s_out_ref[0, 0] = s_new


def _deltanet(proj, ab, conv_prev, state0, conv_w, a_log, dt_bias, norm_w, B, L, *, TL, CP):
    H, dk = DN_HEADS, DN_KDIM
    nl = L // TL
    cq, ck, cv, cz = COL_QDN // dk, COL_KDN // dk, COL_VDN // dk, COL_ZDN // dk
    row = lambda off: pl.BlockSpec((TL, dk), lambda b, h, i: (b * nl + i, off + h))
    cprev = lambda off: pl.BlockSpec((1, DN_CONV - 1, dk), lambda b, h, i: (b, 0, off + h))
    cwt = lambda off: pl.BlockSpec((DN_CONV, dk), lambda b, h, i: (0, off + h))
    per_head = pl.BlockSpec((1, 1, LANES), lambda b, h, i: (h, 0, 0))
    st = pl.BlockSpec((1, 1, dk, DN_VDIM), lambda b, h, i: (b, h, 0, 0))
    alog_b = jnp.broadcast_to(a_log.astype(F32)[:, None, None], (H, 1, LANES))
    dtb_b = jnp.broadcast_to(dt_bias.astype(F32)[:, None, None], (H, 1, LANES))
    o, s = pl.pallas_call(
        functools.partial(_dn_kernel, TL=TL, CP=CP),
        grid=(B, H, nl),
        in_specs=[row(cq), row(ck), row(cv), row(cz),
                  pl.BlockSpec((TL, LANES), lambda b, h, i: (b * nl + i, 0)),
                  cprev(0), cprev(H), cprev(2 * H), cwt(0), cwt(H), cwt(2 * H),
                  st, per_head, per_head,
                  pl.BlockSpec((1, DN_VDIM), lambda b, h, i: (0, 0))],
        out_specs=[pl.BlockSpec((TL, DN_VDIM), lambda b, h, i: (b * nl + i, h)), st],
        out_shape=[jax.ShapeDtypeStruct((B * L, H * DN_VDIM), BF16),
                   jax.ShapeDtypeStruct((B, H, dk, DN_VDIM), F32)],
        scratch_shapes=[pltpu.VMEM((dk, DN_VDIM), F32), pltpu.VMEM((3, 8, dk), F32),
                        pltpu.VMEM((TL + 8, dk), F32)],
        compiler_params=_cparams(("arbitrary",) * 3, 40),
        name="deltanet",
    )(proj, proj, proj, proj, ab, conv_prev, conv_prev, conv_prev, conv_w, conv_w, conv_w,
      state0, alog_b, dtb_b, norm_w.reshape(1, DN_VDIM).astype(F32))
    return o, s


def _xattn_kernel(q_ref, mk_ref, mv_ref, o_ref):
    dx = X_HEAD_DIM
    for h in range(X_HEADS):
        sl = slice(h * dx, (h + 1) * dx)
        s = _dot_g(q_ref[:, sl], mk_ref[0][:, sl], NT) * (dx ** -0.5)
        e = jnp.exp(s - jnp.max(s, axis=-1, keepdims=True))
        p = e / jnp.sum(e, axis=-1, keepdims=True)
        o_ref[:, sl] = _dot(p, mv_ref[0][:, sl]).astype(o_ref.dtype)


def _cross_attn(proj, mem_k, mem_v, B, L, *, tm=512):
    tm = min(tm, L)
    nl = L // tm
    M = mem_k.shape[1]
    return pl.pallas_call(
        _xattn_kernel,
        grid=(B, nl),
        in_specs=[pl.BlockSpec((tm, X_W), lambda b, i: (b * nl + i, COL_QX // X_W)),
                  pl.BlockSpec((1, M, X_W), lambda b, i: (b, 0, 0)),
                  pl.BlockSpec((1, M, X_W), lambda b, i: (b, 0, 0))],
        out_specs=pl.BlockSpec((tm, X_W), lambda b, i: (b * nl + i, 0)),
        out_shape=jax.ShapeDtypeStruct((B * L, X_W), BF16),
        compiler_params=_cparams(("arbitrary", "arbitrary"), 32),
        name="cross_attn",
    )(proj, mem_k, mem_v)


def _merge_kernel(osb_ref, odn_ref, ox_ref, wsb_ref, wdn_ref, wx_ref, gsb_ref, gdn_ref, gx_ref, o_ref):
    acc = _sigmoid(gsb_ref[...]) * _dot(osb_ref[...], wsb_ref[...])
    acc = acc + _sigmoid(gdn_ref[...]) * _dot(odn_ref[...], wdn_ref[...])
    acc = acc + _sigmoid(gx_ref[...]) * _dot(ox_ref[...], wx_ref[...])
    o_ref[...] = acc.astype(o_ref.dtype)


def _merge(o_sb, o_dn, o_x, w_sb_o, w_dn_o, w_x_o, proj, D, *, tm=1024, tn=512):
    T = o_sb.shape[0]
    tm = min(tm, T)
    g0 = (COL_QX + X_W) // tn
    act = lambda w: pl.BlockSpec((tm, w), lambda i, j: (i, 0))
    wt = lambda k: pl.BlockSpec((k, tn), lambda i, j: (0, j))
    gate = lambda n: pl.BlockSpec((tm, tn), lambda i, j: (i, g0 + n * (D // tn) + j))
    return pl.pallas_call(
        _merge_kernel,
        grid=(T // tm, D // tn),
        in_specs=[act(SB_W), act(DN_HEADS * DN_VDIM), act(X_W), wt(SB_W), wt(DN_HEADS * DN_VDIM), wt(X_W),
                  gate(0), gate(1), gate(2)],
        out_specs=pl.BlockSpec((tm, tn), lambda i, j: (i, j)),
        out_shape=jax.ShapeDtypeStruct((T, D), BF16),
        compiler_params=_cparams(("arbitrary", "arbitrary"), 48),
        name="merge",
    )(o_sb, o_dn, o_x, w_sb_o, w_dn_o, w_x_o, proj, proj, proj)


def _out_router_kernel(x_ref, m_ref, wo_ref, nw_ref, wr_ref, br_ref, h_ref, hn_ref, ids_ref, gates_ref):
    hres = x_ref[...] + jnp.dot(m_ref[...], wo_ref[...], preferred_element_type=F32)
    h_ref[...] = hres
    hn = hres * lax.rsqrt(jnp.mean(hres * hres, axis=-1, keepdims=True) + RMS_EPS) * nw_ref[...]
    hn_ref[...] = hn
    logits = _dot3(hn, wr_ref[...]) + br_ref[...]
    lane_i = lax.broadcasted_iota(I32, logits.shape, 1)
    lane = lane_i.astype(F32)
    ninf = -jnp.inf
    first = lambda hit: jnp.min(jnp.where(hit, lane, float(LANES)), axis=-1, keepdims=True)
    gl = jnp.where(lane < N_GROUPS, logits, ninf)
    gmax = jnp.max(gl, axis=-1, keepdims=True)
    gidx = first(gl == gmax)
    g_p = 1.0 / jnp.sum(jnp.where(lane < N_GROUPS, jnp.exp(gl - gmax), 0.0), axis=-1, keepdims=True)
    lo = N_GROUPS + EXPERTS_PER_GROUP * gidx
    el = jnp.where(lane >= lo, jnp.where(lane < lo + EXPERTS_PER_GROUP, logits, ninf), ninf)
    m1 = jnp.max(el, axis=-1, keepdims=True)
    i1 = first(el == m1)
    el2 = jnp.where(lane == i1, ninf, el)
    m2 = jnp.max(el2, axis=-1, keepdims=True)
    i2 = first(el2 == m2)
    e2 = jnp.exp(m2 - m1)
    gate1 = g_p / (1.0 + e2)
    ids = jnp.where(lane_i == 0, i1 - N_GROUPS, jnp.where(lane_i == 1, i2 - N_GROUPS, 0.0))
    ids_ref[...] = ids.astype(I32)
    gates_ref[...] = jnp.where(lane_i == 0, gate1, jnp.where(lane_i == 1, gate1 * e2, 0.0))


def _out_router(x, merged, w_out_bf, nw, w_router, b_router, *, tm=512):
    T, D = x.shape
    tm = min(tm, T)
    rowf = pl.BlockSpec((tm, D), lambda i: (i, 0))
    const = lambda s: pl.BlockSpec(s, lambda i: (0, 0))
    narrow = pl.BlockSpec((tm, LANES), lambda i: (i, 0))
    return pl.pallas_call(
        _out_router_kernel,
        grid=(T // tm,),
        in_specs=[rowf, rowf, const((D, D)), const((1, D)), const((D, LANES)), const((1, LANES))],
        out_specs=[rowf, rowf, narrow, narrow],
        out_shape=[jax.ShapeDtypeStruct((T, D), F32), jax.ShapeDtypeStruct((T, D), F32),
                   jax.ShapeDtypeStruct((T, LANES), I32), jax.ShapeDtypeStruct((T, LANES), F32)],
        compiler_params=_cparams(("arbitrary",), 56),
        name="out_router",
    )(x, merged, w_out_bf, nw.reshape(1, D), w_router, b_router)


def _moe_gather_kernel(tok_ref, hp_ref, hs_ref, o_ref, buf_ref, sem, *, GB, TP):
    base = pl.program_id(0) * GB

    def issue(rr, carry):
        tok = tok_ref[base + rr]

        @pl.when(tok < TP)
        def _():
            pltpu.make_async_copy(hp_ref.at[pl.ds(tok, 1)], buf_ref.at[pl.ds(rr, 1)], sem).start()

        @pl.when(tok >= TP)
        def _():
            pltpu.make_async_copy(hs_ref.at[pl.ds(tok - TP, 1)], buf_ref.at[pl.ds(rr, 1)], sem).start()

        return carry

    lax.fori_loop(0, GB, issue, 0)
    pltpu.make_async_copy(hp_ref.at[pl.ds(0, GB)], buf_ref, sem).wait()
    o_ref[...] = buf_ref[...].astype(o_ref.dtype)


def _moe_gather(slot_tok, hn_p, hn_s, *, GB=256):
    P = slot_tok.shape[0]
    TP, D = hn_p.shape
    return pl.pallas_call(
        functools.partial(_moe_gather_kernel, GB=GB, TP=TP),
        grid_spec=pltpu.PrefetchScalarGridSpec(
            num_scalar_prefetch=1, grid=(P // GB,),
            in_specs=[pl.BlockSpec(memory_space=pl.ANY), pl.BlockSpec(memory_space=pl.ANY)],
            out_specs=pl.BlockSpec((GB, D), lambda i, tok: (i, 0)),
            scratch_shapes=[pltpu.VMEM((GB, D), F32), pltpu.SemaphoreType.DMA(())]),
        out_shape=jax.ShapeDtypeStruct((P, D), BF16),
        compiler_params=_cparams(("arbitrary",), 24),
        name="moe_gather",
    )(slot_tok, hn_p, hn_s)


def _moe_expert_kernel(ie_ref, ib_ref, ins_ref, x_ref, w1_ref, w3_ref, w2_ref, y_ref, acc_ref, *, SUB):
    w, f, s = pl.program_id(0), pl.program_id(1), pl.program_id(2)
    del ie_ref, ib_ref

    @pl.when((f == pl.num_programs(1) - 1) & (s < -ins_ref[w]))
    def _():
        y_ref[...] = jnp.zeros_like(y_ref)

    @pl.when(s < ins_ref[w])
    def _():
        x = x_ref[...]
        hid = _silu(_dot(x, w1_ref[0])) * _dot(x, w3_ref[0])
        part = _dot(hid, w2_ref[0])
        rows = pl.ds(pl.multiple_of(s * SUB, SUB), SUB)

        @pl.when(f == 0)
        def _():
            acc_ref[rows, :] = part

        @pl.when(f > 0)
        def _():
            acc_ref[rows, :] += part

        @pl.when(f == pl.num_programs(1) - 1)
        def _():
            y_ref[...] = acc_ref[rows, :]


def _moe_experts(xs, item_e, item_b, item_n, w1, w3, w2, *, SUB, NSUB, tf=512):
    P, D = xs.shape
    E, _, DE = w1.shape
    NI = item_e.shape[0]
    nf = DE // tf

    def sub_blk(w, s, ib, ins):
        return ib[w] + jnp.minimum(s, jnp.maximum(jnp.abs(ins[w]) - 1, 0))

    return pl.pallas_call(
        functools.partial(_moe_expert_kernel, SUB=SUB),
        grid_spec=pltpu.PrefetchScalarGridSpec(
            num_scalar_prefetch=3, grid=(NI, nf, NSUB),
            in_specs=[
                pl.BlockSpec((SUB, D), lambda w, f, s, ie, ib, ins: (sub_blk(w, s, ib, ins), 0)),
                pl.BlockSpec((1, D, tf), lambda w, f, s, ie, ib, ins: (ie[w], 0, f)),
                pl.BlockSpec((1, D, tf), lambda w, f, s, ie, ib, ins: (ie[w], 0, f)),
                pl.BlockSpec((1, tf, D), lambda w, f, s, ie, ib, ins: (ie[w], f, 0)),
            ],
            out_specs=pl.BlockSpec(
                (SUB, D), lambda w, f, s, ie, ib, ins: (jnp.where(f == nf - 1, sub_blk(w, s, ib, ins), ib[w]), 0)),
            scratch_shapes=[pltpu.VMEM((NSUB * SUB, D), F32)]),
        out_shape=jax.ShapeDtypeStruct((P, D), F32),
        compiler_params=_cparams(("arbitrary",) * 3, 58),
        name="moe_experts",
    )(item_e, item_b, item_n, xs, w1, w3, w2)


def _moe_combine_kernel(pos_ref, h_ref, g_ref, nw_ref, y_hbm, o_ref, b0_ref, b1_ref, sem, *, tm, tok0):
    base = (tok0 + pl.program_id(0) * tm) * TOP_K

    def issue(rr, carry):
        pltpu.make_async_copy(y_hbm.at[pl.ds(pos_ref[base + TOP_K * rr], 1)], b0_ref.at[pl.ds(rr, 1)], sem).start()
        pltpu.make_async_copy(y_hbm.at[pl.ds(pos_ref[base + TOP_K * rr + 1], 1)], b1_ref.at[pl.ds(rr, 1)], sem).start()
        return carry

    lax.fori_loop(0, tm, issue, 0)
    pltpu.make_async_copy(y_hbm.at[pl.ds(0, tm)], b0_ref, sem).wait()
    pltpu.make_async_copy(y_hbm.at[pl.ds(0, tm)], b1_ref, sem).wait()
    g = g_ref[...]
    hf = h_ref[...] + (g[:, 0:1] * b0_ref[...] + g[:, 1:2] * b1_ref[...])
    o_ref[...] = hf * lax.rsqrt(jnp.mean(hf * hf, axis=-1, keepdims=True) + RMS_EPS) * nw_ref[...]


def _moe_combine(pos, h, gates, nw, y, tok0, *, tm=256):
    T, D = h.shape
    tm = min(tm, T)
    return pl.pallas_call(
        functools.partial(_moe_combine_kernel, tm=tm, tok0=tok0),
        grid_spec=pltpu.PrefetchScalarGridSpec(
            num_scalar_prefetch=1, grid=(T // tm,),
            in_specs=[pl.BlockSpec((tm, D), lambda i, p: (i, 0)),
                      pl.BlockSpec((tm, LANES), lambda i, p: (i, 0)),
                      pl.BlockSpec((1, D), lambda i, p: (0, 0)),
                      pl.BlockSpec(memory_space=pl.ANY)],
            out_specs=pl.BlockSpec((tm, D), lambda i, p: (i, 0)),
            scratch_shapes=[pltpu.VMEM((tm, D), F32), pltpu.VMEM((tm, D), F32), pltpu.SemaphoreType.DMA(())]),
        out_shape=jax.ShapeDtypeStruct((T, D), F32),
        compiler_params=_cparams(("arbitrary",), 32),
        name="moe_combine",
    )(pos, h, gates, nw.reshape(1, D), y)


def _moe_plan(e_ids, *, SUB, NSUB):
    A = e_ids.shape[0]
    E = N_EXPERTS
    P = ((A + E * (SUB - 1) + SUB - 1) // SUB) * SUB
    order = jnp.argsort(e_ids, stable=True).astype(I32)
    counts = jnp.zeros((E,), I32).at[e_ids].add(1)
    padded = ((counts + SUB - 1) // SUB) * SUB
    start = jnp.cumsum(counts) - counts
    pstart = jnp.cumsum(padded) - padded
    se = e_ids[order]
    dest_sorted = pstart[se] + jnp.arange(A, dtype=I32) - start[se]
    slot_tok = jnp.zeros((P,), I32).at[dest_sorted].set(order // TOP_K)
    pos = jnp.zeros((A,), I32).at[order].set(dest_sorted)
    nblk = padded // SUB
    n_items = (nblk + NSUB - 1) // NSUB
    iend = jnp.cumsum(n_items)
    NI = E + P // (SUB * NSUB)
    w = jnp.arange(NI, dtype=I32)
    total = iend[-1]
    wc = jnp.minimum(w, total - 1)
    ie = jnp.searchsorted(iend, wc, side="right").astype(I32)
    k = wc - (iend[ie] - n_items[ie])
    ib = pstart[ie] // SUB + k * NSUB
    ins = jnp.minimum(NSUB, nblk[ie] - k * NSUB)
    valid = w < total
    n_blocks = P // SUB
    tail0 = jnp.sum(nblk) + (w - total) * NSUB
    tail_n = jnp.clip(n_blocks - tail0, 0, NSUB)
    return (slot_tok, pos, ie, jnp.where(valid, ib, jnp.minimum(tail0, n_blocks - 1)).astype(I32),
            jnp.where(valid, ins, -tail_n).astype(I32))


def _mixer(x2, B, L, sb_fn, mem_k, mem_v, conv_prev, state0, lw, *, TL, CP):
    (norm_mix_w, w_a, w_b, w_ab, dn_conv_w, dn_a_log, dn_dt_bias, dn_norm_w,
     w_sb_o, w_dn_o, w_x_o, w_out_bf, norm_ffn_w, w_router, b_router) = lw
    D = x2.shape[1]
    proj, ab = _proj(x2, norm_mix_w, w_a, COL_AB, w_b, w_ab)
    o_sb = sb_fn(proj)
    o_dn, dn_state = _deltanet(proj, ab, conv_prev, state0, dn_conv_w, dn_a_log, dn_dt_bias, dn_norm_w,
                               B, L, TL=TL, CP=CP)
    o_x = _cross_attn(proj, mem_k, mem_v, B, L)
    merged = _merge(o_sb, o_dn, o_x, w_sb_o, w_dn_o, w_x_o, proj, D)
    h, hn, ids, gates = _out_router(x2, merged, w_out_bf, norm_ffn_w, w_router, b_router)
    k_new = proj[:, COL_KSB:COL_KSB + SB_W].reshape(B, L, SB_HEADS, SB_HEAD_DIM)
    v_new = proj[:, COL_VSB:COL_VSB + SB_W].reshape(B, L, SB_HEADS, SB_HEAD_DIM)
    qkv = proj[:, COL_QDN:COL_ZDN].reshape(B, L, 3 * DN_W)
    K = DN_CONV - 1
    new_conv = jnp.concatenate([conv_prev, qkv[:, L - min(L, K):]], axis=1)[:, -K:]
    return h, hn, ids, gates, k_new, v_new, dn_state, new_conv


def kernel(x_prompt, x_sample, cache_sb_k, cache_sb_v, cache_mem_k, cache_mem_v, state_dn, state_dn_conv,
           page_table, mem_prompt, norm_mix_w, w_in, sb_bias, dn_conv_w, dn_a_log, dn_dt_bias, dn_norm_w,
           mem_norm_w, w_mem_k, w_mem_v, w_sb_o, w_dn_o, w_x_o, w_out, norm_ffn_w, w_router_g, b_router_g,
           w_router_e, b_router_e, w_e1, w_e3, w_e2, norm_final_w):
    BP, S, D = x_prompt.shape
    DB, Q, _ = x_sample.shape
    depth = w_in.shape[0]
    n_mem = mem_prompt.shape[1]
    TP, TS = BP * S, DB * Q
    SUB, NSUB = 256, 8
    hp, hs = x_prompt.reshape(TP, D), x_sample.reshape(TS, D)
    outs = {k: [] for k in ("sb_kp", "sb_vp", "mem_kp", "mem_vp", "dn_sp", "dn_cp", "sb_ks", "sb_vs", "dn_ss", "dn_cs")}
    for l in range(depth):
        w_ab = jnp.pad(w_in[l][:, COL_AB:COL_AB + 2 * DN_HEADS], ((0, 0), (0, LANES - 2 * DN_HEADS)))
        w_b = w_in[l][:, COL_AB + 2 * DN_HEADS:]
        n_r = N_GROUPS + N_EXPERTS
        w_router = jnp.pad(jnp.concatenate([w_router_g[l], w_router_e[l]], axis=1), ((0, 0), (0, LANES - n_r)))
        b_router = jnp.pad(jnp.concatenate([b_router_g[l], b_router_e[l]]), (0, LANES - n_r)).reshape(1, LANES)
        lw = (norm_mix_w[l], w_in[l], w_b, w_ab, dn_conv_w[l], dn_a_log[l], dn_dt_bias[l], dn_norm_w[l],
              w_sb_o[l], w_dn_o[l], w_x_o[l], w_out[l].astype(BF16), norm_ffn_w[l], w_router, b_router.astype(F32))
        mkv = _proj(mem_prompt.reshape(BP * n_mem, D), mem_norm_w[l], w_mem_k[l], X_W, w_mem_v[l], tm=512)
        mk, mv = mkv[:, :X_W].reshape(BP, n_mem, X_W), mkv[:, X_W:].reshape(BP, n_mem, X_W)
        conv0 = jnp.zeros((BP, DN_CONV - 1, 3 * DN_W), F32)
        s0 = jnp.zeros((BP, DN_HEADS, DN_KDIM, DN_VDIM), F32)
        sbp = functools.partial(_sb_prompt, bias=sb_bias[l].astype(F32), B=BP, S=S)
        hp, hnp_, idp, gp, k_new, v_new, s_new, c_new = _mixer(hp, BP, S, sbp, mk, mv, conv0, s0, lw, TL=256, CP=256)
        outs["sb_kp"].append(k_new); outs["sb_vp"].append(v_new)
        outs["mem_kp"].append(mk.reshape(BP, n_mem, X_HEADS, X_HEAD_DIM))
        outs["mem_vp"].append(mv.reshape(BP, n_mem, X_HEADS, X_HEAD_DIM))
        outs["dn_sp"].append(s_new); outs["dn_cp"].append(c_new)
        sbs = functools.partial(_sb_sample, bias=sb_bias[l].astype(F32), cache_k=cache_sb_k, cache_v=cache_sb_v,
                                page_table=page_table, layer=l, DB=DB, Q=Q)
        hs, hns_, ids_, gs, k_new, v_new, s_new, c_new = _mixer(
            hs, DB, Q, sbs, cache_mem_k[l].reshape(DB, n_mem, X_W), cache_mem_v[l].reshape(DB, n_mem, X_W),
            state_dn_conv[l], state_dn[l], lw, TL=Q, CP=LANES)
        outs["sb_ks"].append(k_new); outs["sb_vs"].append(v_new)
        outs["dn_ss"].append(s_new); outs["dn_cs"].append(c_new)
        e_ids = jnp.concatenate([idp[:, :TOP_K], ids_[:, :TOP_K]], axis=0).reshape(-1)
        slot_tok, pos, ie, ib, ins = _moe_plan(e_ids, SUB=SUB, NSUB=NSUB)
        xs = _moe_gather(slot_tok, hnp_, hns_)
        y = _moe_experts(xs, ie, ib, ins, w_e1[l], w_e3[l], w_e2[l], SUB=SUB, NSUB=NSUB)
        last = l == depth - 1
        nfw = norm_final_w if last else None
        assert last, "multi-layer stacking needs the un-normalised residual; only the final layer applies norm_final"
        hp = _moe_combine(pos, hp, gp, nfw, y, 0)
        hs = _moe_combine(pos, hs, gs, nfw, y, TP)
    stack = lambda k: jnp.stack(outs[k])
    return (hp.reshape(BP, S, D), hs.reshape(DB, Q, D), stack("sb_kp"), stack("sb_vp"), stack("mem_kp"),
            stack("mem_vp"), stack("dn_sp"), stack("dn_cp"), stack("sb_ks"), stack("sb_vs"), stack("dn_ss"),
            stack("dn_cs"))
```

```python
import functools

import jax
import jax.numpy as jnp
from jax import lax
from jax.experimental import pallas as pl
from jax.experimental.pallas import tpu as pltpu

F32, BF16, I32 = jnp.float32, jnp.bfloat16, jnp.int32

RMS_EPS = 1e-6
L2_EPS = 1e-6
SB_HEADS, SB_HEAD_DIM = 8, 128
DN_HEADS, DN_KDIM, DN_VDIM, DN_CONV = 8, 128, 128, 4
X_HEADS, X_HEAD_DIM = 4, 256
N_GROUPS, EXPERTS_PER_GROUP, TOP_K = 4, 8, 2
N_EXPERTS = N_GROUPS * EXPERTS_PER_GROUP
PAGE_SIZE = 128

LANES = 128
MIB = 1024 * 1024
LOG2E = 1.4426950408889634

SB_W = SB_HEADS * SB_HEAD_DIM
DN_W = DN_HEADS * DN_KDIM
X_W = X_HEADS * X_HEAD_DIM
COL_QSB, COL_KSB, COL_VSB = 0, SB_W, 2 * SB_W
COL_QDN = 3 * SB_W
COL_KDN, COL_VDN, COL_ZDN = COL_QDN + DN_W, COL_QDN + 2 * DN_W, COL_QDN + 3 * DN_W
COL_AB = COL_ZDN + DN_W
COL_QX = COL_AB
NT = (((1,), (1,)), ((), ()))
TN = (((0,), (0,)), ((), ()))


def _cparams(sem, vmem_mib):
    return pltpu.CompilerParams(dimension_semantics=sem, vmem_limit_bytes=int(vmem_mib * MIB))


def _dot(a, b):
    return jnp.dot(a.astype(BF16), b.astype(BF16), preferred_element_type=F32)


def _dot_g(a, b, dims):
    return lax.dot_general(a.astype(BF16), b.astype(BF16), dims, preferred_element_type=F32)


def _split2(a):
    hi = a.astype(BF16)
    lo = (a - hi.astype(F32)).astype(BF16)
    return hi, lo


def _dot3(a, b):
    ah, al = _split2(a)
    bh, bl = _split2(b)
    d = functools.partial(jnp.dot, preferred_element_type=F32)
    return d(ah, bh) + (d(ah, bl) + d(al, bh))


def _softplus(z):
    return jnp.maximum(z, 0.0) + jnp.log(1.0 + jnp.exp(-jnp.abs(z)))


def _sigmoid(z):
    return 1.0 / (1.0 + jnp.exp(-z))


def _silu(z):
    return z * _sigmoid(z)


def _iota2(shape):
    return lax.broadcasted_iota(I32, shape, 0), lax.broadcasted_iota(I32, shape, 1)


def _div_pow2(x, p):
    assert p & (p - 1) == 0, p
    return lax.shift_right_logical(x, p.bit_length() - 1)


def _proj_kernel(*refs, na, has_extra):
    if has_extra:
        x_ref, nw_ref, wa_ref, wb_ref, we_ref, o_ref, e_ref, xn_ref = refs
    else:
        x_ref, nw_ref, wa_ref, wb_ref, o_ref, xn_ref = refs
    j = pl.program_id(1)

    @pl.when(j == 0)
    def _():
        x = x_ref[...]
        ms = jnp.mean(x * x, axis=-1, keepdims=True)
        xn = (x * lax.rsqrt(ms + RMS_EPS) * nw_ref[...]).astype(BF16)
        xn_ref[...] = xn
        if has_extra:
            e_ref[...] = jnp.dot(xn, we_ref[...].astype(BF16), preferred_element_type=F32)

    @pl.when(j < na)
    def _():
        o_ref[...] = jnp.dot(xn_ref[...], wa_ref[...].astype(BF16), preferred_element_type=F32)

    @pl.when(j >= na)
    def _():
        o_ref[...] = jnp.dot(xn_ref[...], wb_ref[...].astype(BF16), preferred_element_type=F32)


def _proj(x, nw, wa, na_cols, wb, w_extra=None, *, tm=1024, tn=512):
    T, D = x.shape
    tm = min(tm, T)
    assert T % tm == 0 and na_cols % tn == 0 and wb.shape[1] % tn == 0
    na, nb = na_cols // tn, wb.shape[1] // tn
    has_extra = w_extra is not None
    in_specs = [
        pl.BlockSpec((tm, D), lambda i, j: (i, 0)),
        pl.BlockSpec((1, D), lambda i, j: (0, 0)),
        pl.BlockSpec((D, tn), lambda i, j: (0, jnp.minimum(j, na - 1))),
        pl.BlockSpec((D, tn), lambda i, j: (0, jnp.maximum(j - na, 0))),
    ]
    out_shape = [jax.ShapeDtypeStruct((T, (na + nb) * tn), F32)]
    out_specs = [pl.BlockSpec((tm, tn), lambda i, j: (i, j))]
    args = [x, nw.reshape(1, D), wa, wb]
    if has_extra:
        in_specs.append(pl.BlockSpec((D, LANES), lambda i, j: (0, 0)))
        out_shape.append(jax.ShapeDtypeStruct((T, LANES), F32))
        out_specs.append(pl.BlockSpec((tm, LANES), lambda i, j: (i, 0)))
        args.append(w_extra)
    vmem = (2 * tm * D * 4 + tm * D * 2 + 4 * D * tn * 4 + 2 * tm * tn * 4 + 4 * MIB) / MIB + 6
    res = pl.pallas_call(
        functools.partial(_proj_kernel, na=na, has_extra=has_extra),
        grid=(T // tm, na + nb),
        in_specs=in_specs, out_specs=out_specs, out_shape=out_shape,
        scratch_shapes=[pltpu.VMEM((tm, D), BF16)],
        compiler_params=_cparams(("arbitrary", "arbitrary"), vmem),
        name="proj",
    )(*args)
    return res if has_extra else res[0]


def _sb_weights(z2s, masks, u2, acc):
    sp = [jnp.maximum(z, 0.0) + jnp.log(1.0 + jnp.exp2(jnp.minimum(z, -z))) * LOG2E for z in z2s]
    sp = [s if m is None else jnp.where(m, s, 0.0) for s, m in zip(sp, masks)]
    hl = [jnp.concatenate(_split2(s), axis=1) for s in sp]
    cum = [jnp.dot(x, u2, preferred_element_type=F32) for x in hl]
    es = [z - c for z, c in zip(z2s, cum)]
    ws = []
    for e, c, m in zip(es, cum, masks):
        w = jnp.exp2(e - acc)
        ws.append(w if m is None else jnp.where(m, w, 0.0))
        acc = acc + c[:, 0:1]
    return ws, acc


def _later_key_matrix2(n):
    r, c = _iota2((2 * n, n))
    return jnp.where(jnp.bitwise_and(r, n - 1) >= c, 1.0, 0.0).astype(BF16)


def _sb_prompt_kernel(bias_ref, q_ref, k_ref, v_ref, o_ref, kb_ref, vb_ref, *, tq, scale, unroll):
    h, i = pl.program_id(1), pl.program_id(2)

    @pl.when(i == 0)
    def _():
        kb_ref[...] = k_ref[...].astype(BF16)
        vb_ref[...] = v_ref[...].astype(BF16)

    bias2 = bias_ref[h] * LOG2E
    q = (q_ref[...] * (scale * LOG2E)).astype(BF16)
    u2 = _later_key_matrix2(tq)
    r, c = _iota2((tq, tq))
    strict = c < r

    def run(blocks, masks, o, acc):
        sls = [pl.ds(pl.multiple_of(j * tq, tq), tq) for j in blocks]
        z2s = [lax.dot_general(q, kb_ref[sl, :], NT, preferred_element_type=F32) + bias2 for sl in sls]
        ws, acc = _sb_weights(z2s, masks, u2, acc)
        pvs = [jnp.dot(w.astype(BF16), vb_ref[sl, :], preferred_element_type=F32) for w, sl in zip(ws, sls)]
        while len(pvs) > 1:
            pvs = [a + b for a, b in zip(pvs[::2], pvs[1::2])] + ([pvs[-1]] if len(pvs) % 2 else [])
        return o + pvs[0], acc

    carry = run([i], [strict], jnp.zeros(o_ref.shape, F32), jnp.zeros((tq, 1), F32))
    rem = jnp.bitwise_and(i, unroll - 1)
    carry = lax.fori_loop(0, rem, lambda t, cr: run([i - 1 - t], [None], *cr), carry)
    top = i - 1 - rem

    def body(t, cr):
        j0 = top - t * unroll
        return run([j0 - n for n in range(unroll)], [None] * unroll, *cr)

    o, _ = lax.fori_loop(0, _div_pow2(i, unroll), body, carry)
    o_ref[...] = o.astype(o_ref.dtype)


def _sb_prompt(proj, bias, B, S, *, tq=256, unroll=4):
    H, d = SB_HEADS, SB_HEAD_DIM
    nq = S // tq
    return pl.pallas_call(
        functools.partial(_sb_prompt_kernel, tq=tq, scale=d ** -0.5, unroll=unroll),
        grid=(B, H, nq),
        in_specs=[
            pl.BlockSpec(memory_space=pltpu.SMEM),
            pl.BlockSpec((tq, d), lambda b, h, i: (b * nq + i, COL_QSB // d + h)),
            pl.BlockSpec((S, d), lambda b, h, i: (b, COL_KSB // d + h)),
            pl.BlockSpec((S, d), lambda b, h, i: (b, COL_VSB // d + h)),
        ],
        out_specs=pl.BlockSpec((tq, d), lambda b, h, i: (b * nq + i, h)),
        out_shape=jax.ShapeDtypeStruct((B * S, H * d), BF16),
        scratch_shapes=[pltpu.VMEM((S, d), BF16), pltpu.VMEM((S, d), BF16)],
        compiler_params=_cparams(("arbitrary",) * 3, 4 * S * d * 4 / MIB + 2 * S * d * 2 / MIB + 16),
        name="sb_prompt",
    )(bias, proj, proj, proj)


def _sb_sample_kernel(pt_ref, bias_ref, q_ref, kn_ref, vn_ref, *rest, G, scale):
    del pt_ref
    k_refs, v_refs = rest[:G], rest[G:2 * G]
    o_ref, acc_ref, out_ref = rest[2 * G:]
    H, d, P = SB_HEADS, SB_HEAD_DIM, PAGE_SIZE
    Q = q_ref.shape[0]
    HQ = H * Q
    g = pl.program_id(1)
    u2 = _later_key_matrix2(P)
    qs = [(q_ref[:, h * d:(h + 1) * d] * (scale * LOG2E)).astype(BF16) for h in range(H)]

    def attend(tiles, acc, outs):
        z2s = [jnp.concatenate(
            [lax.dot_general(qs[h], kfn(h).astype(BF16), NT, preferred_element_type=F32) for h in range(H)],
            axis=0) + bias_ref[...] for kfn, _, _ in tiles]
        ws, acc = _sb_weights(z2s, [m for _, _, m in tiles], u2, acc)
        for w, (_, vfn, _) in zip(ws, tiles):
            outs = [outs[h] + jnp.dot(w[h * Q:(h + 1) * Q].astype(BF16), vfn(h).astype(BF16),
                                      preferred_element_type=F32) for h in range(H)]
        return acc, outs

    def finish(acc, outs):
        acc_ref[...] = acc
        out_ref[...] = jnp.concatenate(outs, axis=0)

    @pl.when(g == 0)
    def _():
        pad = jnp.zeros((P - Q, d), F32)
        rq, ck = _iota2((HQ, P))
        new = (lambda h: jnp.concatenate([kn_ref[:, h * d:(h + 1) * d], pad], axis=0),
               lambda h: jnp.concatenate([vn_ref[:, h * d:(h + 1) * d], pad], axis=0),
               ck < jnp.bitwise_and(rq, Q - 1))
        finish(*attend([new], jnp.zeros((HQ, 1), F32), [jnp.zeros((Q, d), F32)] * H))

    tiles = [(lambda h, t=t: k_refs[t][0, pl.ds(h, P, stride=H), :],
              lambda h, t=t: v_refs[t][0, pl.ds(h, P, stride=H), :], None) for t in range(G)]
    finish(*attend(tiles, acc_ref[...], [out_ref[h * Q:(h + 1) * Q, :] for h in range(H)]))

    @pl.when(g == pl.num_programs(1) - 1)
    def _():
        for h in range(H):
            o_ref[:, h * d:(h + 1) * d] = out_ref[h * Q:(h + 1) * Q, :].astype(o_ref.dtype)


def _sb_sample(proj, bias, cache_k, cache_v, page_table, layer, DB, Q, *, G=8):
    H, d = SB_HEADS, SB_HEAD_DIM
    W = H * d
    depth, n_pool = cache_k.shape[:2]
    NP = page_table.shape[1]
    ck = cache_k.reshape(depth * n_pool, PAGE_SIZE * H, d)
    cv = cache_v.reshape(depth * n_pool, PAGE_SIZE * H, d)
    base = layer * n_pool
    bias_rows = jnp.broadcast_to(jnp.repeat(bias * LOG2E, Q)[:, None], (H * Q, PAGE_SIZE)).astype(F32)

    def page_spec(t):
        return pl.BlockSpec((1, PAGE_SIZE * H, d), lambda b, g, pt: (base + pt[b, NP - 1 - (g * G + t)], 0, 0))

    in_specs = [
        pl.BlockSpec((H * Q, PAGE_SIZE), lambda b, g, pt: (0, 0)),
        pl.BlockSpec((Q, W), lambda b, g, pt: (b, COL_QSB // W)),
        pl.BlockSpec((Q, W), lambda b, g, pt: (b, COL_KSB // W)),
        pl.BlockSpec((Q, W), lambda b, g, pt: (b, COL_VSB // W)),
    ] + [page_spec(t) for t in range(G)] * 2
    return pl.pallas_call(
        functools.partial(_sb_sample_kernel, G=G, scale=d ** -0.5),
        grid_spec=pltpu.PrefetchScalarGridSpec(
            num_scalar_prefetch=1, grid=(DB, NP // G),
            in_specs=in_specs,
            out_specs=pl.BlockSpec((Q, W), lambda b, g, pt: (b, 0)),
            scratch_shapes=[pltpu.VMEM((H * Q, 1), F32), pltpu.VMEM((H * Q, d), F32)]),
        out_shape=jax.ShapeDtypeStruct((DB * Q, W), BF16),
        compiler_params=_cparams(("arbitrary", "arbitrary"), 4 * G * PAGE_SIZE * W * 4 / MIB + 12),
        name="sb_sample",
    )(page_table, bias_rows, proj, proj, proj, *([ck] * G), *([cv] * G))


def _unit_lower_inverse_m1(lms, n):
    r, c = _iota2((n, n))
    same = lambda s: _div_pow2(r, s) == _div_pow2(c, s)
    each = lambda f, *ls: [f(*a) for a in zip(*ls)]
    l16 = [jnp.where(same(16), lm, 0.0) for lm in lms]
    l2 = each(_dot, l16, l16)
    l4 = each(_dot, l2, l2)
    l8 = each(_dot, l4, l4)
    xr = [-a for a in l16]
    for p in (l2, l4, l8):
        xr = each(lambda x, q: x + q + _dot(x, q), xr, p)
    size = 32
    while size <= n:
        block = jnp.where(same(size), jnp.where(same(size // 2), 0.0, 1.0), 0.0)
        y = each(lambda x, lm: lm * block + _dot(x, lm * block), xr, lms)
        xr = each(lambda x, yy: x - yy - _dot(yy, x), xr, y)
        size *= 2
    return xr


def _dot2(a, b):
    ah, al = _split2(a)
    bh = b.astype(BF16)
    return jnp.dot(ah, bh, preferred_element_type=F32) + jnp.dot(al, bh, preferred_element_type=F32)


def _dn_kernel(q_ref, k_ref, v_ref, z_ref, ab_ref, cpq_ref, cpk_ref, cpv_ref, cwq_ref, cwk_ref, cwv_ref,
               s0_ref, alog_ref, dtb_ref, nw_ref, o_ref, s_out_ref, s_ref, prev_ref, seq_ref, *, TL, CP, HB):
    hg, i = pl.program_id(1), pl.program_id(2)
    K = DN_CONV - 1
    dk = DN_KDIM

    @pl.when(i == 0)
    def _():
        s_ref[...] = s0_ref[0]
        prev_ref[...] = jnp.zeros_like(prev_ref)
        for n, cp in enumerate((cpq_ref, cpk_ref, cpv_ref)):
            prev_ref[n, 8 - K:8, :] = cp[0]

    def conv(n, x_ref, cw_ref):
        seq_ref[0:8, :] = prev_ref[n]
        seq_ref[8:8 + TL, :] = x_ref[...]
        cw = cw_ref[...]
        acc = seq_ref[8:8 + TL, :] * cw[K:K + 1, :]
        for s in range(1, DN_CONV):
            acc = acc + seq_ref[8 - s:8 - s + TL, :] * cw[K - s:K - s + 1, :]
        prev_ref[n] = seq_ref[TL:TL + 8, :]
        y = _silu(acc)
        if CP > TL:
            y = jnp.concatenate([y, jnp.zeros((CP - TL, y.shape[1]), F32)], axis=0)
        return y

    qc_all = conv(0, q_ref, cwq_ref)
    kc_all = conv(1, k_ref, cwk_ref)
    vc_all = conv(2, v_ref, cwv_ref)
    ab = ab_ref[...]
    lane = lax.broadcasted_iota(I32, ab.shape, 1)
    r, c = _iota2((CP, CP))
    causal = r >= c
    tri = jnp.where(causal, 1.0, 0.0).astype(BF16)
    d = functools.partial(jnp.dot, preferred_element_type=F32)

    heads = range(HB)
    each = lambda f, *ls: [f(*a) for a in zip(*ls)]
    sls = [slice(hh * dk, (hh + 1) * dk) for hh in heads]
    l2n = lambda t: t * lax.rsqrt(jnp.sum(t * t, axis=-1, keepdims=True) + L2_EPS)
    qn = [l2n(qc_all[:, sl]) * (DN_KDIM ** -0.5) for sl in sls]
    kn = [l2n(kc_all[:, sl]) for sl in sls]
    vc = [vc_all[:, sl] for sl in sls]
    pick = lambda col: jnp.sum(jnp.where(lane == col, ab, 0.0), axis=-1, keepdims=True)
    pad0 = (lambda t: jnp.concatenate([t, jnp.zeros((CP - TL, 1), F32)], axis=0)) if CP > TL else (lambda t: t)
    beta = [pad0(_sigmoid(pick(DN_HEADS + hg * HB + hh))) for hh in heads]
    gl = [pad0(-jnp.exp(alog_ref[hg * HB + hh][:, 0:1]) * _softplus(pick(hg * HB + hh) + dtb_ref[hg * HB + hh][:, 0:1]))
          for hh in heads]

    def cumulative(g):
        g1 = jnp.broadcast_to(g, (CP, LANES))
        h1 = g1.astype(BF16)
        r1 = g1 - h1.astype(F32)
        h2 = r1.astype(BF16)
        h3 = (r1 - h2.astype(F32)).astype(BF16)
        return d(tri, h1) + (d(tri, h2) + d(tri, h3))

    gcum_b = [cumulative(g) for g in gl]
    gc = [t[:, 0:1] for t in gcum_b]
    gr = [jnp.transpose(t)[0:1, :] for t in gcum_b]
    decay = each(lambda a, b: jnp.where(causal, jnp.exp(jnp.where(causal, a - b, 0.0)), 0.0), gc, gr)
    kb = each(lambda k, b: k * b, kn, beta)
    lm = each(lambda a, b, dc: jnp.where(r > c, _dot_g(a, b, NT) * dc, 0.0), kb, kn, decay)
    qk = each(lambda a, b, dc: _dot_g(a, b, NT) * dc, qn, kn, decay)
    xr = _unit_lower_inverse_m1(lm, CP)
    eg = [jnp.exp(t) for t in gc]
    solve = lambda x, rhs: rhs + _dot2(x, rhs)
    u = each(lambda x, v, b: solve(x, v * b), xr, vc, beta)
    w = each(lambda x, k, e: solve(x, k * e), xr, kb, eg)
    s = [s_ref[hh] for hh in heads]
    v_new = each(lambda uu, ww, ss: uu - _dot(ww, ss), u, w, s)
    o = each(lambda q, e, ss, a, vn: _dot(q * e, ss) + _dot(a, vn), qn, eg, s, qk, v_new)
    g_last = [t[CP - 1:CP, :] for t in gc]
    s_new = each(lambda ss, gz, k, g, vn: ss * jnp.exp(gz) + _dot_g(k * jnp.exp(gz - g), vn, TN),
                 s, g_last, kn, gc, v_new)
    for hh in heads:
        s_ref[hh] = s_new[hh]
        oh = o[hh][:TL]
        oh = oh * lax.rsqrt(jnp.mean(oh * oh, axis=-1, keepdims=True) + RMS_EPS) * nw_ref[...]
        o_ref[:, sls[hh]] = (oh * _silu(z_ref[:, sls[hh]])).astype(o_ref.dtype)

    @pl.when(i == pl.num_programs(2) - 1)
    def _():
        s_out_ref[0] = s_ref[...]


def _deltanet(proj, ab, conv_prev, state0, conv_w, a_log, dt_bias, norm_w, B, L, *, TL, CP, HB):
    H, dk = DN_HEADS, DN_KDIM
    nl = L // TL
    wb = HB * dk
    cq, ck, cv, cz = COL_QDN // wb, COL_KDN // wb, COL_VDN // wb, COL_ZDN // wb
    row = lambda off: pl.BlockSpec((TL, wb), lambda b, h, i: (b * nl + i, off + h))
    cprev = lambda off: pl.BlockSpec((1, DN_CONV - 1, wb), lambda b, h, i: (b, 0, off + h))
    cwt = lambda off: pl.BlockSpec((DN_CONV, wb), lambda b, h, i: (0, off + h))
    per_head = pl.BlockSpec((H, 1, LANES), lambda b, h, i: (0, 0, 0))
    st = pl.BlockSpec((1, HB, dk, DN_VDIM), lambda b, h, i: (b, h, 0, 0))
    alog_b = jnp.broadcast_to(a_log.astype(F32)[:, None, None], (H, 1, LANES))
    dtb_b = jnp.broadcast_to(dt_bias.astype(F32)[:, None, None], (H, 1, LANES))
    o, s = pl.pallas_call(
        functools.partial(_dn_kernel, TL=TL, CP=CP, HB=HB),
        grid=(B, H // HB, nl),
        in_specs=[row(cq), row(ck), row(cv), row(cz),
                  pl.BlockSpec((TL, LANES), lambda b, h, i: (b * nl + i, 0)),
                  cprev(0), cprev(H // HB), cprev(2 * H // HB), cwt(0), cwt(H // HB), cwt(2 * H // HB),
                  st, per_head, per_head,
                  pl.BlockSpec((1, DN_VDIM), lambda b, h, i: (0, 0))],
        out_specs=[pl.BlockSpec((TL, wb), lambda b, h, i: (b * nl + i, h)), st],
        out_shape=[jax.ShapeDtypeStruct((B * L, H * DN_VDIM), BF16),
                   jax.ShapeDtypeStruct((B, H, dk, DN_VDIM), F32)],
        scratch_shapes=[pltpu.VMEM((HB, dk, DN_VDIM), F32), pltpu.VMEM((3, 8, wb), F32),
                        pltpu.VMEM((TL + 8, wb), F32)],
        compiler_params=_cparams(("arbitrary",) * 3, 48),
        name="deltanet",
    )(proj, proj, proj, proj, ab, conv_prev, conv_prev, conv_prev, conv_w, conv_w, conv_w,
      state0, alog_b, dtb_b, norm_w.reshape(1, DN_VDIM).astype(F32))
    return o, s


def _xattn_kernel(q_ref, mk_ref, mv_ref, o_ref):
    dx = X_HEAD_DIM
    for h in range(X_HEADS):
        sl = slice(h * dx, (h + 1) * dx)
        s = _dot_g(q_ref[:, sl], mk_ref[0][:, sl], NT) * (dx ** -0.5)
        e = jnp.exp(s - jnp.max(s, axis=-1, keepdims=True))
        p = e / jnp.sum(e, axis=-1, keepdims=True)
        o_ref[:, sl] = _dot(p, mv_ref[0][:, sl]).astype(o_ref.dtype)


def _cross_attn(proj, mem_k, mem_v, B, L, *, tm=512):
    tm = min(tm, L)
    nl = L // tm
    M = mem_k.shape[1]
    return pl.pallas_call(
        _xattn_kernel,
        grid=(B, nl),
        in_specs=[pl.BlockSpec((tm, X_W), lambda b, i: (b * nl + i, COL_QX // X_W)),
                  pl.BlockSpec((1, M, X_W), lambda b, i: (b, 0, 0)),
                  pl.BlockSpec((1, M, X_W), lambda b, i: (b, 0, 0))],
        out_specs=pl.BlockSpec((tm, X_W), lambda b, i: (b * nl + i, 0)),
        out_shape=jax.ShapeDtypeStruct((B * L, X_W), BF16),
        compiler_params=_cparams(("arbitrary", "arbitrary"), 32),
        name="cross_attn",
    )(proj, mem_k, mem_v)


def _merge_kernel(osb_ref, odn_ref, ox_ref, wsb_ref, wdn_ref, wx_ref, gsb_ref, gdn_ref, gx_ref, o_ref):
    acc = _sigmoid(gsb_ref[...]) * _dot(osb_ref[...], wsb_ref[...])
    acc = acc + _sigmoid(gdn_ref[...]) * _dot(odn_ref[...], wdn_ref[...])
    acc = acc + _sigmoid(gx_ref[...]) * _dot(ox_ref[...], wx_ref[...])
    o_ref[...] = acc.astype(o_ref.dtype)


def _merge(o_sb, o_dn, o_x, w_sb_o, w_dn_o, w_x_o, proj, D, *, tm=1024, tn=512):
    T = o_sb.shape[0]
    tm = min(tm, T)
    assert T % tm == 0 and D % tn == 0
    g0 = (COL_QX + X_W) // tn
    act = lambda w: pl.BlockSpec((tm, w), lambda i, j: (i, 0))
    wt = lambda k: pl.BlockSpec((k, tn), lambda i, j: (0, j))
    gate = lambda n: pl.BlockSpec((tm, tn), lambda i, j: (i, g0 + n * (D // tn) + j))
    return pl.pallas_call(
        _merge_kernel,
        grid=(T // tm, D // tn),
        in_specs=[act(SB_W), act(DN_HEADS * DN_VDIM), act(X_W), wt(SB_W), wt(DN_HEADS * DN_VDIM), wt(X_W),
                  gate(0), gate(1), gate(2)],
        out_specs=pl.BlockSpec((tm, tn), lambda i, j: (i, j)),
        out_shape=jax.ShapeDtypeStruct((T, D), BF16),
        compiler_params=_cparams(("arbitrary", "arbitrary"), 48),
        name="merge",
    )(o_sb, o_dn, o_x, w_sb_o, w_dn_o, w_x_o, proj, proj, proj)


def _out_router_kernel(x_ref, m_ref, wo_ref, nw_ref, wr_ref, br_ref, *rest, n_main):
    h_ref, hn_ref, ids_ref, gates_ref = rest[-4:]

    @pl.when(pl.program_id(0) >= n_main)
    def _():
        hn_ref[...] = jnp.zeros_like(hn_ref)

    @pl.when(pl.program_id(0) < n_main)
    def _():
        _out_router_body(x_ref, m_ref, wo_ref, nw_ref, wr_ref, br_ref, h_ref, hn_ref, ids_ref, gates_ref)


def _out_router_body(x_ref, m_ref, wo_ref, nw_ref, wr_ref, br_ref, h_ref, hn_ref, ids_ref, gates_ref):
    hres = x_ref[...] + jnp.dot(m_ref[...], wo_ref[...], preferred_element_type=F32)
    h_ref[...] = hres
    hn = hres * lax.rsqrt(jnp.mean(hres * hres, axis=-1, keepdims=True) + RMS_EPS) * nw_ref[...]
    hn_ref[...] = hn
    logits = _dot3(hn, wr_ref[...]) + br_ref[...]
    lane_i = lax.broadcasted_iota(I32, logits.shape, 1)
    lane = lane_i.astype(F32)
    ninf = -jnp.inf
    first = lambda hit: jnp.min(jnp.where(hit, lane, float(LANES)), axis=-1, keepdims=True)
    gl = jnp.where(lane < N_GROUPS, logits, ninf)
    gmax = jnp.max(gl, axis=-1, keepdims=True)
    gidx = first(gl == gmax)
    g_p = 1.0 / jnp.sum(jnp.where(lane < N_GROUPS, jnp.exp(gl - gmax), 0.0), axis=-1, keepdims=True)
    lo = N_GROUPS + EXPERTS_PER_GROUP * gidx
    el = jnp.where(lane >= lo, jnp.where(lane < lo + EXPERTS_PER_GROUP, logits, ninf), ninf)
    m1 = jnp.max(el, axis=-1, keepdims=True)
    i1 = first(el == m1)
    el2 = jnp.where(lane == i1, ninf, el)
    m2 = jnp.max(el2, axis=-1, keepdims=True)
    i2 = first(el2 == m2)
    e2 = jnp.exp(m2 - m1)
    gate1 = g_p / (1.0 + e2)
    ids = jnp.where(lane_i == 0, i1 - N_GROUPS, jnp.where(lane_i == 1, i2 - N_GROUPS, 0.0))
    ids_ref[...] = ids.astype(I32)
    gates_ref[...] = jnp.where(lane_i == 0, gate1, jnp.where(lane_i == 1, gate1 * e2, 0.0))


def _out_router(x, merged, w_out_bf, nw, w_router, b_router, hn_all, row0, *, tm=512):
    T, D = x.shape
    tm = min(tm, T)
    assert T % tm == 0 and row0 % tm == 0
    n_main = T // tm
    rowf = pl.BlockSpec((tm, D), lambda i: (jnp.minimum(i, n_main - 1), 0))
    const = lambda s: pl.BlockSpec(s, lambda i: (0, 0))
    narrow = pl.BlockSpec((tm, LANES), lambda i: (jnp.minimum(i, n_main - 1), 0))
    in_specs = [rowf, rowf, const((D, D)), const((1, D)), const((D, LANES)), const((1, LANES))]
    args = [x, merged, w_out_bf, nw.reshape(1, D), w_router, b_router]
    if isinstance(hn_all, int):
        assert row0 == 0
        t_all, aliases = pl.cdiv(hn_all, tm) * tm, {}
    else:
        t_all, aliases = hn_all.shape[0], {len(args): 1}
        in_specs.append(pl.BlockSpec(memory_space=pl.ANY))
        args.append(hn_all)
        assert t_all % tm == 0 and row0 + T <= t_all
    n_steps = n_main if aliases else t_all // tm
    return pl.pallas_call(
        functools.partial(_out_router_kernel, n_main=n_main),
        grid=(n_steps,),
        in_specs=in_specs,
        out_specs=[rowf, pl.BlockSpec((tm, D), lambda i: (row0 // tm + i, 0)), narrow, narrow],
        out_shape=[jax.ShapeDtypeStruct((T, D), F32), jax.ShapeDtypeStruct((t_all, D), F32),
                   jax.ShapeDtypeStruct((T, LANES), I32), jax.ShapeDtypeStruct((T, LANES), F32)],
        input_output_aliases=aliases,
        compiler_params=_cparams(("arbitrary",), 56),
        name="out_router",
    )(*args)


def _moe_gather_kernel(tok_ref, hn_ref, o_ref, buf_ref, sem, *, GB):
    i, n = pl.program_id(0), pl.num_programs(0)
    slot = jnp.bitwise_and(i, 1)

    def start(blk, sl):
        def issue(rr, carry):
            tok = tok_ref[blk * GB + rr]
            pltpu.make_async_copy(hn_ref.at[pl.ds(tok, 1)], buf_ref.at[sl, pl.ds(rr, 1)], sem.at[sl]).start()
            return carry

        lax.fori_loop(0, GB, issue, 0, unroll=8)

    @pl.when(i == 0)
    def _():
        start(0, 0)

    @pl.when(i + 1 < n)
    def _():
        start(i + 1, 1 - slot)

    pltpu.make_async_copy(hn_ref.at[pl.ds(0, GB)], buf_ref.at[slot], sem.at[slot]).wait()
    o_ref[...] = buf_ref[slot].astype(o_ref.dtype)


def _moe_gather(slot_tok, hn, *, GB=512):
    P = slot_tok.shape[0]
    D = hn.shape[1]
    return pl.pallas_call(
        functools.partial(_moe_gather_kernel, GB=GB),
        grid_spec=pltpu.PrefetchScalarGridSpec(
            num_scalar_prefetch=1, grid=(P // GB,),
            in_specs=[pl.BlockSpec(memory_space=pl.ANY)],
            out_specs=pl.BlockSpec((GB, D), lambda i, tok: (i, 0)),
            scratch_shapes=[pltpu.VMEM((2, GB, D), F32), pltpu.SemaphoreType.DMA((2,))]),
        out_shape=jax.ShapeDtypeStruct((P, D), BF16),
        compiler_params=_cparams(("arbitrary",), 24),
        name="moe_gather",
    )(slot_tok, hn)


def _moe_expert_kernel(ie_ref, ib_ref, ins_ref, x_ref, w1_ref, w3_ref, w2_ref, y_ref, acc_ref, *, SUB):
    w, f, s = pl.program_id(0), pl.program_id(1), pl.program_id(2)
    del ie_ref, ib_ref

    @pl.when((f == pl.num_programs(1) - 1) & (s < -ins_ref[w]))
    def _():
        y_ref[...] = jnp.zeros_like(y_ref)

    @pl.when(s < ins_ref[w])
    def _():
        x = x_ref[...]
        hid = _silu(_dot(x, w1_ref[0])) * _dot(x, w3_ref[0])
        part = _dot(hid, w2_ref[0])
        rows = pl.ds(pl.multiple_of(s * SUB, SUB), SUB)

        @pl.when(f == 0)
        def _():
            acc_ref[rows, :] = part

        @pl.when(f > 0)
        def _():
            acc_ref[rows, :] += part

        @pl.when(f == pl.num_programs(1) - 1)
        def _():
            y_ref[...] = acc_ref[rows, :]


def _moe_experts(xs, item_e, item_b, item_n, w1, w3, w2, *, SUB, NSUB, tf=512):
    P, D = xs.shape
    E, _, DE = w1.shape
    NI = item_e.shape[0]
    nf = DE // tf

    def sub_blk(w, s, ib, ins):
        return ib[w] + jnp.minimum(s, jnp.maximum(jnp.abs(ins[w]) - 1, 0))

    return pl.pallas_call(
        functools.partial(_moe_expert_kernel, SUB=SUB),
        grid_spec=pltpu.PrefetchScalarGridSpec(
            num_scalar_prefetch=3, grid=(NI, nf, NSUB),
            in_specs=[
                pl.BlockSpec((SUB, D), lambda w, f, s, ie, ib, ins: (sub_blk(w, s, ib, ins), 0)),
                pl.BlockSpec((1, D, tf), lambda w, f, s, ie, ib, ins: (ie[w], 0, f)),
                pl.BlockSpec((1, D, tf), lambda w, f, s, ie, ib, ins: (ie[w], 0, f)),
                pl.BlockSpec((1, tf, D), lambda w, f, s, ie, ib, ins: (ie[w], f, 0)),
            ],
            out_specs=pl.BlockSpec(
                (SUB, D), lambda w, f, s, ie, ib, ins: (jnp.where(f == nf - 1, sub_blk(w, s, ib, ins), ib[w]), 0)),
            scratch_shapes=[pltpu.VMEM((NSUB * SUB, D), F32)]),
        out_shape=jax.ShapeDtypeStruct((P, D), F32),
        compiler_params=_cparams(("arbitrary",) * 3, 58),
        name="moe_experts",
    )(item_e, item_b, item_n, xs, w1, w3, w2)


def _moe_combine_kernel(pos_ref, h_ref, g_ref, nw_ref, y_hbm, o_ref, b0_ref, b1_ref, sem, *, tm, tok0):
    base = (tok0 + pl.program_id(0) * tm) * TOP_K

    def issue(rr, carry):
        pltpu.make_async_copy(y_hbm.at[pl.ds(pos_ref[base + TOP_K * rr], 1)], b0_ref.at[pl.ds(rr, 1)], sem).start()
        pltpu.make_async_copy(y_hbm.at[pl.ds(pos_ref[base + TOP_K * rr + 1], 1)], b1_ref.at[pl.ds(rr, 1)], sem).start()
        return carry

    lax.fori_loop(0, tm, issue, 0, unroll=4)
    pltpu.make_async_copy(y_hbm.at[pl.ds(0, tm)], b0_ref, sem).wait()
    pltpu.make_async_copy(y_hbm.at[pl.ds(0, tm)], b1_ref, sem).wait()
    g = g_ref[...]
    hf = h_ref[...] + (g[:, 0:1] * b0_ref[...] + g[:, 1:2] * b1_ref[...])
    o_ref[...] = hf * lax.rsqrt(jnp.mean(hf * hf, axis=-1, keepdims=True) + RMS_EPS) * nw_ref[...]


def _moe_combine(pos, h, gates, nw, y, tok0, *, tm=256):
    T, D = h.shape
    tm = min(tm, T)
    return pl.pallas_call(
        functools.partial(_moe_combine_kernel, tm=tm, tok0=tok0),
        grid_spec=pltpu.PrefetchScalarGridSpec(
            num_scalar_prefetch=1, grid=(T // tm,),
            in_specs=[pl.BlockSpec((tm, D), lambda i, p: (i, 0)),
                      pl.BlockSpec((tm, LANES), lambda i, p: (i, 0)),
                      pl.BlockSpec((1, D), lambda i, p: (0, 0)),
                      pl.BlockSpec(memory_space=pl.ANY)],
            out_specs=pl.BlockSpec((tm, D), lambda i, p: (i, 0)),
            scratch_shapes=[pltpu.VMEM((tm, D), F32), pltpu.VMEM((tm, D), F32), pltpu.SemaphoreType.DMA(())]),
        out_shape=jax.ShapeDtypeStruct((T, D), F32),
        compiler_params=_cparams(("arbitrary",), 32),
        name="moe_combine",
    )(pos, h, gates, nw.reshape(1, D), y)


def _moe_plan(e_ids, *, SUB, NSUB):
    A = e_ids.shape[0]
    E = N_EXPERTS
    P = ((A + E * (SUB - 1) + SUB - 1) // SUB) * SUB
    order = jnp.argsort(e_ids, stable=True).astype(I32)
    counts = jnp.sum((e_ids[:, None] == jnp.arange(E, dtype=I32)[None, :]).astype(I32), axis=0)
    padded = ((counts + SUB - 1) // SUB) * SUB
    start = jnp.cumsum(counts) - counts
    pend = jnp.cumsum(padded)
    pstart = pend - padded
    se = e_ids[order]
    dest_sorted = pstart[se] + jnp.arange(A, dtype=I32) - start[se]
    pos = dest_sorted[jnp.argsort(order).astype(I32)]
    p = jnp.arange(P, dtype=I32)
    pe = jnp.minimum(jnp.sum((p[:, None] >= pend[None, :]).astype(I32), axis=1), E - 1)
    rank = p - pstart[pe]
    src = order[jnp.clip(start[pe] + rank, 0, A - 1)] // TOP_K
    slot_tok = jnp.where(rank < counts[pe], src, 0).astype(I32)
    nblk = padded // SUB
    n_items = (nblk + NSUB - 1) // NSUB
    iend = jnp.cumsum(n_items)
    n_blocks = P // SUB
    NI = E + (n_blocks + NSUB - 1) // NSUB
    w = jnp.arange(NI, dtype=I32)
    total = iend[-1]
    wc = jnp.minimum(w, total - 1)
    ie = jnp.minimum(jnp.sum((wc[:, None] >= iend[None, :]).astype(I32), axis=1), E - 1)
    k = wc - (iend[ie] - n_items[ie])
    ib = pstart[ie] // SUB + k * NSUB
    ins = jnp.minimum(NSUB, nblk[ie] - k * NSUB)
    valid = w < total
    tail0 = jnp.sum(nblk) + (w - total) * NSUB
    tail_n = jnp.clip(n_blocks - tail0, 0, NSUB)
    return (slot_tok, pos, ie.astype(I32), jnp.where(valid, ib, jnp.minimum(tail0, n_blocks - 1)).astype(I32),
            jnp.where(valid, ins, -tail_n).astype(I32))


def _mixer(x2, B, L, sb_fn, mem_k, mem_v, conv_prev, state0, lw, hn_all, row0, *, TL, CP, HB):
    (norm_mix_w, w_a, w_b, w_ab, dn_conv_w, dn_a_log, dn_dt_bias, dn_norm_w,
     w_sb_o, w_dn_o, w_x_o, w_out_bf, norm_ffn_w, w_router, b_router) = lw
    D = x2.shape[1]
    proj, ab = _proj(x2, norm_mix_w, w_a, COL_AB, w_b, w_ab)
    o_sb = sb_fn(proj)
    o_dn, dn_state = _deltanet(proj, ab, conv_prev, state0, dn_conv_w, dn_a_log, dn_dt_bias, dn_norm_w,
                               B, L, TL=TL, CP=CP, HB=HB)
    o_x = _cross_attn(proj, mem_k, mem_v, B, L)
    merged = _merge(o_sb, o_dn, o_x, w_sb_o, w_dn_o, w_x_o, proj, D)
    h, hn, ids, gates = _out_router(x2, merged, w_out_bf, norm_ffn_w, w_router, b_router, hn_all, row0)
    k_new = proj[:, COL_KSB:COL_KSB + SB_W].reshape(B, L, SB_HEADS, SB_HEAD_DIM)
    v_new = proj[:, COL_VSB:COL_VSB + SB_W].reshape(B, L, SB_HEADS, SB_HEAD_DIM)
    K = DN_CONV - 1
    tail = proj.reshape(B, L, proj.shape[1])[:, L - min(L, K):, COL_QDN:COL_ZDN]
    new_conv = jnp.concatenate([conv_prev, tail], axis=1)[:, -K:]
    return h, hn, ids, gates, k_new, v_new, dn_state, new_conv


def kernel(x_prompt, x_sample, cache_sb_k, cache_sb_v, cache_mem_k, cache_mem_v, state_dn, state_dn_conv,
           page_table, mem_prompt, norm_mix_w, w_in, sb_bias, dn_conv_w, dn_a_log, dn_dt_bias, dn_norm_w,
           mem_norm_w, w_mem_k, w_mem_v, w_sb_o, w_dn_o, w_x_o, w_out, norm_ffn_w, w_router_g, b_router_g,
           w_router_e, b_router_e, w_e1, w_e3, w_e2, norm_final_w):
    BP, S, D = x_prompt.shape
    DB, Q, _ = x_sample.shape
    depth = w_in.shape[0]
    n_mem = mem_prompt.shape[1]
    TP, TS = BP * S, DB * Q
    SUB, NSUB = 512, 3
    hp, hs = x_prompt.reshape(TP, D), x_sample.reshape(TS, D)
    outs = {k: [] for k in ("sb_kp", "sb_vp", "mem_kp", "mem_vp", "dn_sp", "dn_cp", "sb_ks", "sb_vs", "dn_ss", "dn_cs")}
    for l in range(depth):
        w_ab = jnp.pad(w_in[l][:, COL_AB:COL_AB + 2 * DN_HEADS], ((0, 0), (0, LANES - 2 * DN_HEADS)))
        w_b = w_in[l][:, COL_AB + 2 * DN_HEADS:]
        n_r = N_GROUPS + N_EXPERTS
        w_router = jnp.pad(jnp.concatenate([w_router_g[l], w_router_e[l]], axis=1), ((0, 0), (0, LANES - n_r)))
        b_router = jnp.pad(jnp.concatenate([b_router_g[l], b_router_e[l]]), (0, LANES - n_r)).reshape(1, LANES)
        lw = (norm_mix_w[l], w_in[l], w_b, w_ab, dn_conv_w[l], dn_a_log[l], dn_dt_bias[l], dn_norm_w[l],
              w_sb_o[l], w_dn_o[l], w_x_o[l], w_out[l].astype(BF16), norm_ffn_w[l], w_router, b_router.astype(F32))
        mkv = _proj(mem_prompt.reshape(BP * n_mem, D), mem_norm_w[l], w_mem_k[l], X_W, w_mem_v[l], tm=512)
        mk, mv = mkv[:, :X_W].reshape(BP, n_mem, X_W), mkv[:, X_W:].reshape(BP, n_mem, X_W)
        conv0 = jnp.zeros((BP, DN_CONV - 1, 3 * DN_W), F32)
        s0 = jnp.zeros((BP, DN_HEADS, DN_KDIM, DN_VDIM), F32)
        sbp = functools.partial(_sb_prompt, bias=sb_bias[l].astype(F32), B=BP, S=S)
        hp, hn_all, idp, gp, k_new, v_new, s_new, c_new = _mixer(hp, BP, S, sbp, mk, mv, conv0, s0, lw, TP + TS, 0,
                                                                 TL=256, CP=256, HB=4)
        outs["sb_kp"].append(k_new); outs["sb_vp"].append(v_new)
        outs["mem_kp"].append(mk.reshape(BP, n_mem, X_HEADS, X_HEAD_DIM))
        outs["mem_vp"].append(mv.reshape(BP, n_mem, X_HEADS, X_HEAD_DIM))
        outs["dn_sp"].append(s_new); outs["dn_cp"].append(c_new)
        sbs = functools.partial(_sb_sample, bias=sb_bias[l].astype(F32), cache_k=cache_sb_k, cache_v=cache_sb_v,
                                page_table=page_table, layer=l, DB=DB, Q=Q)
        hs, hn_all, ids_, gs, k_new, v_new, s_new, c_new = _mixer(
            hs, DB, Q, sbs, cache_mem_k[l].reshape(DB, n_mem, X_W), cache_mem_v[l].reshape(DB, n_mem, X_W),
            state_dn_conv[l], state_dn[l], lw, hn_all, TP, TL=Q, CP=LANES, HB=DN_HEADS)
        outs["sb_ks"].append(k_new); outs["sb_vs"].append(v_new)
        outs["dn_ss"].append(s_new); outs["dn_cs"].append(c_new)
        e_ids = jnp.concatenate([idp[:, :TOP_K], ids_[:, :TOP_K]], axis=0).reshape(-1)
        slot_tok, pos, ie, ib, ins = _moe_plan(e_ids, SUB=SUB, NSUB=NSUB)
        xs = _moe_gather(slot_tok, hn_all)
        y = _moe_experts(xs, ie, ib, ins, w_e1[l], w_e3[l], w_e2[l], SUB=SUB, NSUB=NSUB)
        last = l == depth - 1
        nfw = norm_final_w if last else None
        assert last, "multi-layer stacking needs the un-normalised residual; only the final layer applies norm_final"
        hp = _moe_combine(pos, hp, gp, nfw, y, 0)
        hs = _moe_combine(pos, hs, gs, nfw, y, TP)
    stack = lambda k: jnp.stack(outs[k])
    return (hp.reshape(BP, S, D), hs.reshape(DB, Q, D), stack("sb_kp"), stack("sb_vp"), stack("mem_kp"),
            stack("mem_vp"), stack("dn_sp"), stack("dn_cp"), stack("sb_ks"), stack("sb_vs"), stack("dn_ss"),
            stack("dn_cs"))
```

```python
import functools

import jax
import jax.numpy as jnp
from jax import lax
from jax.experimental import pallas as pl
from jax.experimental.pallas import tpu as pltpu

F32, BF16, I32 = jnp.float32, jnp.bfloat16, jnp.int32

RMS_EPS = 1e-6
L2_EPS = 1e-6
SB_HEADS, SB_HEAD_DIM = 8, 128
DN_HEADS, DN_KDIM, DN_VDIM, DN_CONV = 8, 128, 128, 4
X_HEADS, X_HEAD_DIM = 4, 256
N_GROUPS, EXPERTS_PER_GROUP, TOP_K = 4, 8, 2
N_EXPERTS = N_GROUPS * EXPERTS_PER_GROUP
PAGE_SIZE = 128

LANES = 128
ISSUE_UNROLL = 8
MIB = 1024 * 1024
LOG2E = 1.4426950408889634

SB_W = SB_HEADS * SB_HEAD_DIM
DN_W = DN_HEADS * DN_KDIM
X_W = X_HEADS * X_HEAD_DIM
COL_QSB, COL_KSB, COL_VSB = 0, SB_W, 2 * SB_W
COL_QDN = 3 * SB_W
COL_KDN, COL_VDN, COL_ZDN = COL_QDN + DN_W, COL_QDN + 2 * DN_W, COL_QDN + 3 * DN_W
COL_AB = COL_ZDN + DN_W
COL_QX = COL_AB
NT = (((1,), (1,)), ((), ()))
TN = (((0,), (0,)), ((), ()))


def _cparams(sem, vmem_mib):
    return pltpu.CompilerParams(dimension_semantics=sem, vmem_limit_bytes=int(vmem_mib * MIB))


def _dot(a, b):
    return jnp.dot(a.astype(BF16), b.astype(BF16), preferred_element_type=F32)


def _dot_g(a, b, dims):
    return lax.dot_general(a.astype(BF16), b.astype(BF16), dims, preferred_element_type=F32)


def _split2(a):
    hi = a.astype(BF16)
    lo = (a - hi.astype(F32)).astype(BF16)
    return hi, lo


def _dot3(a, b):
    ah, al = _split2(a)
    bh, bl = _split2(b)
    d = functools.partial(jnp.dot, preferred_element_type=F32)
    return d(ah, bh) + (d(ah, bl) + d(al, bh))


def _softplus(z):
    return jnp.maximum(z, 0.0) + jnp.log(1.0 + jnp.exp(-jnp.abs(z)))


def _sigmoid(z):
    return 1.0 / (1.0 + jnp.exp(-z))


def _silu(z):
    return z * _sigmoid(z)


def _iota2(shape):
    return lax.broadcasted_iota(I32, shape, 0), lax.broadcasted_iota(I32, shape, 1)


def _div_pow2(x, p):
    assert p & (p - 1) == 0, p
    return lax.shift_right_logical(x, p.bit_length() - 1)


def _proj_kernel(*refs, na, has_extra):
    if has_extra:
        x_ref, nw_ref, wa_ref, wb_ref, we_ref, o_ref, e_ref, xn_ref = refs
    else:
        x_ref, nw_ref, wa_ref, wb_ref, o_ref, xn_ref = refs
    j = pl.program_id(1)

    @pl.when(j == 0)
    def _():
        x = x_ref[...]
        ms = jnp.mean(x * x, axis=-1, keepdims=True)
        xn = (x * lax.rsqrt(ms + RMS_EPS) * nw_ref[...]).astype(BF16)
        xn_ref[...] = xn
        if has_extra:
            e_ref[...] = jnp.dot(xn, we_ref[...].astype(BF16), preferred_element_type=F32)

    @pl.when(j < na)
    def _():
        o_ref[...] = jnp.dot(xn_ref[...], wa_ref[...].astype(BF16), preferred_element_type=F32)

    @pl.when(j >= na)
    def _():
        o_ref[...] = jnp.dot(xn_ref[...], wb_ref[...].astype(BF16), preferred_element_type=F32)


def _proj(x, nw, wa, na_cols, wb, w_extra=None, *, tm=1024, tn=1024):
    T, D = x.shape
    tm = min(tm, T)
    assert T % tm == 0 and na_cols % tn == 0 and wb.shape[1] % tn == 0
    na, nb = na_cols // tn, wb.shape[1] // tn
    has_extra = w_extra is not None
    in_specs = [
        pl.BlockSpec((tm, D), lambda i, j: (i, 0)),
        pl.BlockSpec((1, D), lambda i, j: (0, 0)),
        pl.BlockSpec((D, tn), lambda i, j: (0, jnp.minimum(j, na - 1))),
        pl.BlockSpec((D, tn), lambda i, j: (0, jnp.maximum(j - na, 0))),
    ]
    out_shape = [jax.ShapeDtypeStruct((T, (na + nb) * tn), F32)]
    out_specs = [pl.BlockSpec((tm, tn), lambda i, j: (i, j))]
    args = [x, nw.reshape(1, D), wa, wb]
    if has_extra:
        in_specs.append(pl.BlockSpec((D, LANES), lambda i, j: (0, 0)))
        out_shape.append(jax.ShapeDtypeStruct((T, LANES), F32))
        out_specs.append(pl.BlockSpec((tm, LANES), lambda i, j: (i, 0)))
        args.append(w_extra)
    wbytes = 2 * D * tn * (wa.dtype.itemsize + wb.dtype.itemsize)
    vmem = (2 * tm * D * 4 + tm * D * 2 + wbytes + 2 * tm * tn * 4 + 4 * MIB) / MIB + 6
    res = pl.pallas_call(
        functools.partial(_proj_kernel, na=na, has_extra=has_extra),
        grid=(T // tm, na + nb),
        in_specs=in_specs, out_specs=out_specs, out_shape=out_shape,
        scratch_shapes=[pltpu.VMEM((tm, D), BF16)],
        compiler_params=_cparams(("arbitrary", "arbitrary"), vmem),
        name="proj",
    )(*args)
    return res if has_extra else res[0]


def _sb_weights(z2s, masks, u2, acc):
    sp = [jnp.maximum(z, 0.0) + jnp.log(1.0 + jnp.exp2(jnp.minimum(z, -z))) * LOG2E for z in z2s]
    sp = [s if m is None else jnp.where(m, s, 0.0) for s, m in zip(sp, masks)]
    cum = [jnp.dot(s.astype(BF16), u2, preferred_element_type=F32) for s in sp]
    es = [z - c for z, c in zip(z2s, cum)]
    ws = []
    for e, c, m in zip(es, cum, masks):
        w = jnp.exp2(e - acc)
        ws.append(w if m is None else jnp.where(m, w, 0.0))
        acc = acc + c[:, 0:1]
    return ws, acc


def _later_key_matrix(n):
    r, c = _iota2((n, n))
    return jnp.where(r >= c, 1.0, 0.0).astype(BF16)


def _sb_prompt_kernel(bias_ref, q_ref, k_ref, v_ref, o_ref, kb_ref, vb_ref, *, tq, scale, unroll):
    h, i = pl.program_id(1), pl.program_id(2)

    @pl.when(i == 0)
    def _():
        kb_ref[...] = k_ref[...].astype(BF16)
        vb_ref[...] = v_ref[...].astype(BF16)

    bias2 = bias_ref[h] * LOG2E
    q = (q_ref[...] * (scale * LOG2E)).astype(BF16)
    u2 = _later_key_matrix(tq)
    r, c = _iota2((tq, tq))
    strict = c < r

    def run(blocks, masks, o, acc):
        sls = [pl.ds(pl.multiple_of(j * tq, tq), tq) for j in blocks]
        z2s = [lax.dot_general(q, kb_ref[sl, :], NT, preferred_element_type=F32) + bias2 for sl in sls]
        ws, acc = _sb_weights(z2s, masks, u2, acc)
        pvs = [jnp.dot(w.astype(BF16), vb_ref[sl, :], preferred_element_type=F32) for w, sl in zip(ws, sls)]
        while len(pvs) > 1:
            pvs = [a + b for a, b in zip(pvs[::2], pvs[1::2])] + ([pvs[-1]] if len(pvs) % 2 else [])
        return o + pvs[0], acc

    carry = run([i], [strict], jnp.zeros(o_ref.shape, F32), jnp.zeros((tq, 1), F32))
    rem = jnp.bitwise_and(i, unroll - 1)
    carry = lax.fori_loop(0, rem, lambda t, cr: run([i - 1 - t], [None], *cr), carry)
    top = i - 1 - rem

    def body(t, cr):
        j0 = top - t * unroll
        return run([j0 - n for n in range(unroll)], [None] * unroll, *cr)

    o, _ = lax.fori_loop(0, _div_pow2(i, unroll), body, carry)
    o_ref[...] = o.astype(o_ref.dtype)


def _sb_prompt(proj, bias, B, S, *, tq=256, unroll=4):
    H, d = SB_HEADS, SB_HEAD_DIM
    nq = S // tq
    return pl.pallas_call(
        functools.partial(_sb_prompt_kernel, tq=tq, scale=d ** -0.5, unroll=unroll),
        grid=(B, H, nq),
        in_specs=[
            pl.BlockSpec(memory_space=pltpu.SMEM),
            pl.BlockSpec((tq, d), lambda b, h, i: (b * nq + i, COL_QSB // d + h)),
            pl.BlockSpec((S, d), lambda b, h, i: (b, COL_KSB // d + h)),
            pl.BlockSpec((S, d), lambda b, h, i: (b, COL_VSB // d + h)),
        ],
        out_specs=pl.BlockSpec((tq, d), lambda b, h, i: (b * nq + i, h)),
        out_shape=jax.ShapeDtypeStruct((B * S, H * d), BF16),
        scratch_shapes=[pltpu.VMEM((S, d), BF16), pltpu.VMEM((S, d), BF16)],
        compiler_params=_cparams(("arbitrary",) * 3, 4 * S * d * 4 / MIB + 2 * S * d * 2 / MIB + 16),
        name="sb_prompt",
    )(bias, proj, proj, proj)


def _sb_sample_kernel(pt_ref, bias_ref, q_ref, kn_ref, vn_ref, *rest, G, scale):
    del pt_ref
    k_refs, v_refs = rest[:G], rest[G:2 * G]
    o_ref, acc_ref, out_ref = rest[2 * G:]
    H, d, P = SB_HEADS, SB_HEAD_DIM, PAGE_SIZE
    Q = q_ref.shape[0]
    HQ = H * Q
    g = pl.program_id(1)
    u2 = _later_key_matrix(P)
    qs = [(q_ref[:, h * d:(h + 1) * d] * (scale * LOG2E)).astype(BF16) for h in range(H)]

    def attend(tiles, acc, outs):
        z2s = [jnp.concatenate(
            [lax.dot_general(qs[h], kfn(h).astype(BF16), NT, preferred_element_type=F32) for h in range(H)],
            axis=0) + bias_ref[...] for kfn, _, _ in tiles]
        ws, acc = _sb_weights(z2s, [m for _, _, m in tiles], u2, acc)
        for w, (_, vfn, _) in zip(ws, tiles):
            outs = [outs[h] + jnp.dot(w[h * Q:(h + 1) * Q].astype(BF16), vfn(h).astype(BF16),
                                      preferred_element_type=F32) for h in range(H)]
        return acc, outs

    def finish(acc, outs):
        acc_ref[...] = acc
        out_ref[...] = jnp.concatenate(outs, axis=0)

    @pl.when(g == 0)
    def _():
        pad = jnp.zeros((P - Q, d), F32)
        rq, ck = _iota2((HQ, P))
        new = (lambda h: jnp.concatenate([kn_ref[:, h * d:(h + 1) * d], pad], axis=0),
               lambda h: jnp.concatenate([vn_ref[:, h * d:(h + 1) * d], pad], axis=0),
               ck < jnp.bitwise_and(rq, Q - 1))
        finish(*attend([new], jnp.zeros((HQ, 1), F32), [jnp.zeros((Q, d), F32)] * H))

    tiles = [(lambda h, t=t: k_refs[t][0, pl.ds(h, P, stride=H), :],
              lambda h, t=t: v_refs[t][0, pl.ds(h, P, stride=H), :], None) for t in range(G)]
    finish(*attend(tiles, acc_ref[...], [out_ref[h * Q:(h + 1) * Q, :] for h in range(H)]))

    @pl.when(g == pl.num_programs(1) - 1)
    def _():
        for h in range(H):
            o_ref[:, h * d:(h + 1) * d] = out_ref[h * Q:(h + 1) * Q, :].astype(o_ref.dtype)


def _sb_sample(proj, bias, cache_k, cache_v, page_table, layer, DB, Q, *, G=8):
    H, d = SB_HEADS, SB_HEAD_DIM
    W = H * d
    depth, n_pool = cache_k.shape[:2]
    NP = page_table.shape[1]
    ck = cache_k.reshape(depth * n_pool, PAGE_SIZE * H, d)
    cv = cache_v.reshape(depth * n_pool, PAGE_SIZE * H, d)
    base = layer * n_pool
    bias_rows = jnp.broadcast_to(jnp.repeat(bias * LOG2E, Q)[:, None], (H * Q, PAGE_SIZE)).astype(F32)

    def page_spec(t):
        return pl.BlockSpec((1, PAGE_SIZE * H, d), lambda b, g, pt: (base + pt[b, NP - 1 - (g * G + t)], 0, 0))

    in_specs = [
        pl.BlockSpec((H * Q, PAGE_SIZE), lambda b, g, pt: (0, 0)),
        pl.BlockSpec((Q, W), lambda b, g, pt: (b, COL_QSB // W)),
        pl.BlockSpec((Q, W), lambda b, g, pt: (b, COL_KSB // W)),
        pl.BlockSpec((Q, W), lambda b, g, pt: (b, COL_VSB // W)),
    ] + [page_spec(t) for t in range(G)] * 2
    return pl.pallas_call(
        functools.partial(_sb_sample_kernel, G=G, scale=d ** -0.5),
        grid_spec=pltpu.PrefetchScalarGridSpec(
            num_scalar_prefetch=1, grid=(DB, NP // G),
            in_specs=in_specs,
            out_specs=pl.BlockSpec((Q, W), lambda b, g, pt: (b, 0)),
            scratch_shapes=[pltpu.VMEM((H * Q, 1), F32), pltpu.VMEM((H * Q, d), F32)]),
        out_shape=jax.ShapeDtypeStruct((DB * Q, W), BF16),
        compiler_params=_cparams(("arbitrary", "arbitrary"), 4 * G * PAGE_SIZE * W * 4 / MIB + 12),
        name="sb_sample",
    )(page_table, bias_rows, proj, proj, proj, *([ck] * G), *([cv] * G))


def _unit_lower_inverse_m1(lms, n):
    r, c = _iota2((n, n))
    same = lambda s: _div_pow2(r, s) == _div_pow2(c, s)
    each = lambda f, *ls: [f(*a) for a in zip(*ls)]
    l16 = [jnp.where(same(16), lm, 0.0) for lm in lms]
    l2 = each(_dot, l16, l16)
    l4 = each(_dot, l2, l2)
    l8 = each(_dot, l4, l4)
    xr = [-a for a in l16]
    for p in (l2, l4, l8):
        xr = each(lambda x, q: x + q + _dot(x, q), xr, p)
    size = 32
    while size <= n:
        block = jnp.where(same(size), jnp.where(same(size // 2), 0.0, 1.0), 0.0)
        y = each(lambda x, lm: lm * block + _dot(x, lm * block), xr, lms)
        xr = each(lambda x, yy: x - yy - _dot(yy, x), xr, y)
        size *= 2
    return xr


def _dot2(a, b):
    ah, al = _split2(a)
    bh = b.astype(BF16)
    return jnp.dot(ah, bh, preferred_element_type=F32) + jnp.dot(al, bh, preferred_element_type=F32)


def _dn_kernel(q_ref, k_ref, v_ref, z_ref, ab_ref, cpq_ref, cpk_ref, cpv_ref, cwq_ref, cwk_ref, cwv_ref,
               s0_ref, alog_ref, dtb_ref, nw_ref, o_ref, s_out_ref, s_ref, prev_ref, seq_ref, *, TL, CP, HB):
    hg, i = pl.program_id(1), pl.program_id(2)
    K = DN_CONV - 1
    dk = DN_KDIM

    @pl.when(i == 0)
    def _():
        s_ref[...] = s0_ref[0]
        prev_ref[...] = jnp.zeros_like(prev_ref)
        for n, cp in enumerate((cpq_ref, cpk_ref, cpv_ref)):
            prev_ref[n, 8 - K:8, :] = cp[0]

    def conv(n, x_ref, cw_ref):
        seq_ref[0:8, :] = prev_ref[n]
        seq_ref[8:8 + TL, :] = x_ref[...]
        cw = cw_ref[...]
        acc = seq_ref[8:8 + TL, :] * cw[K:K + 1, :]
        for s in range(1, DN_CONV):
            acc = acc + seq_ref[8 - s:8 - s + TL, :] * cw[K - s:K - s + 1, :]
        prev_ref[n] = seq_ref[TL:TL + 8, :]
        y = _silu(acc)
        if CP > TL:
            y = jnp.concatenate([y, jnp.zeros((CP - TL, y.shape[1]), F32)], axis=0)
        return y

    qc_all = conv(0, q_ref, cwq_ref)
    kc_all = conv(1, k_ref, cwk_ref)
    vc_all = conv(2, v_ref, cwv_ref)
    ab = ab_ref[...]
    lane = lax.broadcasted_iota(I32, ab.shape, 1)
    r, c = _iota2((CP, CP))
    causal = r >= c
    tri = jnp.where(causal, 1.0, 0.0).astype(BF16)
    d = functools.partial(jnp.dot, preferred_element_type=F32)

    heads = range(HB)
    each = lambda f, *ls: [f(*a) for a in zip(*ls)]
    sls = [slice(hh * dk, (hh + 1) * dk) for hh in heads]
    l2n = lambda t: t * lax.rsqrt(jnp.sum(t * t, axis=-1, keepdims=True) + L2_EPS)
    qn = [l2n(qc_all[:, sl]) * (DN_KDIM ** -0.5) for sl in sls]
    kn = [l2n(kc_all[:, sl]) for sl in sls]
    vc = [vc_all[:, sl] for sl in sls]
    pick = lambda col: jnp.sum(jnp.where(lane == col, ab, 0.0), axis=-1, keepdims=True)
    pad0 = (lambda t: jnp.concatenate([t, jnp.zeros((CP - TL, 1), F32)], axis=0)) if CP > TL else (lambda t: t)
    beta = [pad0(_sigmoid(pick(DN_HEADS + hg * HB + hh))) for hh in heads]
    gl = [pad0(-jnp.exp(alog_ref[hg * HB + hh][:, 0:1]) * _softplus(pick(hg * HB + hh) + dtb_ref[hg * HB + hh][:, 0:1]))
          for hh in heads]

    def cumulative(g):
        g1 = jnp.broadcast_to(g, (CP, LANES))
        h1 = g1.astype(BF16)
        r1 = g1 - h1.astype(F32)
        h2 = r1.astype(BF16)
        h3 = (r1 - h2.astype(F32)).astype(BF16)
        return d(tri, h1) + (d(tri, h2) + d(tri, h3))

    gcum_b = [cumulative(g) for g in gl]
    gc = [t[:, 0:1] for t in gcum_b]
    gr = [jnp.transpose(t)[0:1, :] for t in gcum_b]
    decay = each(lambda a, b: jnp.where(causal, jnp.exp(jnp.where(causal, a - b, 0.0)), 0.0), gc, gr)
    kb = each(lambda k, b: k * b, kn, beta)
    lm = each(lambda a, b, dc: jnp.where(r > c, _dot_g(a, b, NT) * dc, 0.0), kb, kn, decay)
    qk = each(lambda a, b, dc: _dot_g(a, b, NT) * dc, qn, kn, decay)
    xr = _unit_lower_inverse_m1(lm, CP)
    eg = [jnp.exp(t) for t in gc]
    solve = lambda x, rhs: rhs + _dot2(x, rhs)
    u = each(lambda x, v, b: solve(x, v * b), xr, vc, beta)
    w = each(lambda x, k, e: solve(x, k * e), xr, kb, eg)
    s = [s_ref[hh] for hh in heads]
    v_new = each(lambda uu, ww, ss: uu - _dot(ww, ss), u, w, s)
    o = each(lambda q, e, ss, a, vn: _dot(q * e, ss) + _dot(a, vn), qn, eg, s, qk, v_new)
    g_last = [t[CP - 1:CP, :] for t in gc]
    s_new = each(lambda ss, gz, k, g, vn: ss * jnp.exp(gz) + _dot_g(k * jnp.exp(gz - g), vn, TN),
                 s, g_last, kn, gc, v_new)
    for hh in heads:
        s_ref[hh] = s_new[hh]
        oh = o[hh][:TL]
        oh = oh * lax.rsqrt(jnp.mean(oh * oh, axis=-1, keepdims=True) + RMS_EPS) * nw_ref[...]
        o_ref[:, sls[hh]] = (oh * _silu(z_ref[:, sls[hh]])).astype(o_ref.dtype)

    @pl.when(i == pl.num_programs(2) - 1)
    def _():
        s_out_ref[0] = s_ref[...]


def _deltanet(proj, ab, conv_prev, state0, conv_w, a_log, dt_bias, norm_w, B, L, *, TL, CP, HB):
    H, dk = DN_HEADS, DN_KDIM
    nl = L // TL
    wb = HB * dk
    cq, ck, cv, cz = COL_QDN // wb, COL_KDN // wb, COL_VDN // wb, COL_ZDN // wb
    row = lambda off: pl.BlockSpec((TL, wb), lambda b, h, i: (b * nl + i, off + h))
    cprev = lambda off: pl.BlockSpec((1, DN_CONV - 1, wb), lambda b, h, i: (b, 0, off + h))
    cwt = lambda off: pl.BlockSpec((DN_CONV, wb), lambda b, h, i: (0, off + h))
    per_head = pl.BlockSpec((H, 1, LANES), lambda b, h, i: (0, 0, 0))
    st = pl.BlockSpec((1, HB, dk, DN_VDIM), lambda b, h, i: (b, h, 0, 0))
    alog_b = jnp.broadcast_to(a_log.astype(F32)[:, None, None], (H, 1, LANES))
    dtb_b = jnp.broadcast_to(dt_bias.astype(F32)[:, None, None], (H, 1, LANES))
    o, s = pl.pallas_call(
        functools.partial(_dn_kernel, TL=TL, CP=CP, HB=HB),
        grid=(B, H // HB, nl),
        in_specs=[row(cq), row(ck), row(cv), row(cz),
                  pl.BlockSpec((TL, LANES), lambda b, h, i: (b * nl + i, 0)),
                  cprev(0), cprev(H // HB), cprev(2 * H // HB), cwt(0), cwt(H // HB), cwt(2 * H // HB),
                  st, per_head, per_head,
                  pl.BlockSpec((1, DN_VDIM), lambda b, h, i: (0, 0))],
        out_specs=[pl.BlockSpec((TL, wb), lambda b, h, i: (b * nl + i, h)), st],
        out_shape=[jax.ShapeDtypeStruct((B * L, H * DN_VDIM), BF16),
                   jax.ShapeDtypeStruct((B, H, dk, DN_VDIM), F32)],
        scratch_shapes=[pltpu.VMEM((HB, dk, DN_VDIM), F32), pltpu.VMEM((3, 8, wb), F32),
                        pltpu.VMEM((TL + 8, wb), F32)],
        compiler_params=_cparams(("arbitrary",) * 3, 48),
        name="deltanet",
    )(proj, proj, proj, proj, ab, conv_prev, conv_prev, conv_prev, conv_w, conv_w, conv_w,
      state0, alog_b, dtb_b, norm_w.reshape(1, DN_VDIM).astype(F32))
    return o, s


def _xattn_kernel(q_ref, mk_ref, mv_ref, o_ref):
    dx = X_HEAD_DIM
    for h in range(X_HEADS):
        sl = slice(h * dx, (h + 1) * dx)
        s = _dot_g(q_ref[:, sl], mk_ref[0][:, sl], NT) * (dx ** -0.5)
        e = jnp.exp(s - jnp.max(s, axis=-1, keepdims=True))
        p = e / jnp.sum(e, axis=-1, keepdims=True)
        o_ref[:, sl] = _dot(p, mv_ref[0][:, sl]).astype(o_ref.dtype)


def _cross_attn(proj, mem_k, mem_v, B, L, *, tm=512):
    tm = min(tm, L)
    nl = L // tm
    M = mem_k.shape[1]
    return pl.pallas_call(
        _xattn_kernel,
        grid=(B, nl),
        in_specs=[pl.BlockSpec((tm, X_W), lambda b, i: (b * nl + i, COL_QX // X_W)),
                  pl.BlockSpec((1, M, X_W), lambda b, i: (b, 0, 0)),
                  pl.BlockSpec((1, M, X_W), lambda b, i: (b, 0, 0))],
        out_specs=pl.BlockSpec((tm, X_W), lambda b, i: (b * nl + i, 0)),
        out_shape=jax.ShapeDtypeStruct((B * L, X_W), BF16),
        compiler_params=_cparams(("arbitrary", "arbitrary"), 32),
        name="cross_attn",
    )(proj, mem_k, mem_v)


def _merge_kernel(osb_ref, odn_ref, ox_ref, wsb_ref, wdn_ref, wx_ref, gsb_ref, gdn_ref, gx_ref, o_ref):
    acc = _sigmoid(gsb_ref[...]) * _dot(osb_ref[...], wsb_ref[...])
    acc = acc + _sigmoid(gdn_ref[...]) * _dot(odn_ref[...], wdn_ref[...])
    acc = acc + _sigmoid(gx_ref[...]) * _dot(ox_ref[...], wx_ref[...])
    o_ref[...] = acc.astype(o_ref.dtype)


def _merge(o_sb, o_dn, o_x, w_sb_o, w_dn_o, w_x_o, proj, D, *, tm=1024, tn=512):
    T = o_sb.shape[0]
    tm = min(tm, T)
    assert T % tm == 0 and D % tn == 0
    g0 = (COL_QX + X_W) // tn
    act = lambda w: pl.BlockSpec((tm, w), lambda i, j: (i, 0))
    wt = lambda k: pl.BlockSpec((k, tn), lambda i, j: (0, j))
    gate = lambda n: pl.BlockSpec((tm, tn), lambda i, j: (i, g0 + n * (D // tn) + j))
    return pl.pallas_call(
        _merge_kernel,
        grid=(T // tm, D // tn),
        in_specs=[act(SB_W), act(DN_HEADS * DN_VDIM), act(X_W), wt(SB_W), wt(DN_HEADS * DN_VDIM), wt(X_W),
                  gate(0), gate(1), gate(2)],
        out_specs=pl.BlockSpec((tm, tn), lambda i, j: (i, j)),
        out_shape=jax.ShapeDtypeStruct((T, D), BF16),
        compiler_params=_cparams(("arbitrary", "arbitrary"), 48),
        name="merge",
    )(o_sb, o_dn, o_x, w_sb_o, w_dn_o, w_x_o, proj, proj, proj)


def _out_router_kernel(x_ref, m_ref, wo_ref, nw_ref, wr_ref, br_ref, *rest, n_main):
    h_ref, hn_ref, ids_ref, gates_ref = rest[-4:]

    @pl.when(pl.program_id(0) >= n_main)
    def _():
        hn_ref[...] = jnp.zeros_like(hn_ref)

    @pl.when(pl.program_id(0) < n_main)
    def _():
        _out_router_body(x_ref, m_ref, wo_ref, nw_ref, wr_ref, br_ref, h_ref, hn_ref, ids_ref, gates_ref)


def _out_router_body(x_ref, m_ref, wo_ref, nw_ref, wr_ref, br_ref, h_ref, hn_ref, ids_ref, gates_ref):
    hres = x_ref[...] + jnp.dot(m_ref[...], wo_ref[...], preferred_element_type=F32)
    h_ref[...] = hres
    hn = hres * lax.rsqrt(jnp.mean(hres * hres, axis=-1, keepdims=True) + RMS_EPS) * nw_ref[...]
    for j in range(hn_ref.shape[1]):
        hn_ref[:, j, :] = hn[:, j * LANES:(j + 1) * LANES]
    logits = _dot3(hn, wr_ref[...]) + br_ref[...]
    lane_i = lax.broadcasted_iota(I32, logits.shape, 1)
    lane = lane_i.astype(F32)
    ninf = -jnp.inf
    first = lambda hit: jnp.min(jnp.where(hit, lane, float(LANES)), axis=-1, keepdims=True)
    gl = jnp.where(lane < N_GROUPS, logits, ninf)
    gmax = jnp.max(gl, axis=-1, keepdims=True)
    gidx = first(gl == gmax)
    g_p = 1.0 / jnp.sum(jnp.where(lane < N_GROUPS, jnp.exp(gl - gmax), 0.0), axis=-1, keepdims=True)
    lo = N_GROUPS + EXPERTS_PER_GROUP * gidx
    el = jnp.where(lane >= lo, jnp.where(lane < lo + EXPERTS_PER_GROUP, logits, ninf), ninf)
    m1 = jnp.max(el, axis=-1, keepdims=True)
    i1 = first(el == m1)
    el2 = jnp.where(lane == i1, ninf, el)
    m2 = jnp.max(el2, axis=-1, keepdims=True)
    i2 = first(el2 == m2)
    e2 = jnp.exp(m2 - m1)
    gate1 = g_p / (1.0 + e2)
    ids = jnp.where(lane_i == 0, i1 - N_GROUPS, jnp.where(lane_i == 1, i2 - N_GROUPS, 0.0))
    ids_ref[...] = ids.astype(I32)
    gates_ref[...] = jnp.where(lane_i == 0, gate1, jnp.where(lane_i == 1, gate1 * e2, 0.0))


def _out_router(x, merged, w_out_bf, nw, w_router, b_router, hn_all, row0, *, tm=512):
    T, D = x.shape
    tm = min(tm, T)
    assert T % tm == 0 and row0 % tm == 0
    n_main = T // tm
    rowf = pl.BlockSpec((tm, D), lambda i: (jnp.minimum(i, n_main - 1), 0))
    const = lambda s: pl.BlockSpec(s, lambda i: (0, 0))
    narrow = pl.BlockSpec((tm, LANES), lambda i: (jnp.minimum(i, n_main - 1), 0))
    in_specs = [rowf, rowf, const((D, D)), const((1, D)), const((D, LANES)), const((1, LANES))]
    args = [x, merged, w_out_bf, nw.reshape(1, D), w_router, b_router]
    if isinstance(hn_all, int):
        assert row0 == 0
        t_all, aliases = pl.cdiv(hn_all, tm) * tm, {}
    else:
        t_all, aliases = hn_all.shape[0], {len(args): 1}
        in_specs.append(pl.BlockSpec(memory_space=pl.ANY))
        args.append(hn_all)
        assert t_all % tm == 0 and row0 + T <= t_all
    n_steps = n_main if aliases else t_all // tm
    return pl.pallas_call(
        functools.partial(_out_router_kernel, n_main=n_main),
        grid=(n_steps,),
        in_specs=in_specs,
        out_specs=[rowf, pl.BlockSpec((tm, D // LANES, LANES), lambda i: (row0 // tm + i, 0, 0)), narrow, narrow],
        out_shape=[jax.ShapeDtypeStruct((T, D), F32), jax.ShapeDtypeStruct((t_all, D // LANES, LANES), F32),
                   jax.ShapeDtypeStruct((T, LANES), I32), jax.ShapeDtypeStruct((T, LANES), F32)],
        input_output_aliases=aliases,
        compiler_params=_cparams(("arbitrary",), 56),
        name="out_router",
    )(*args)


def _moe_gather_kernel(tok_ref, hn_ref, o_ref, buf_ref, sem, *, GB):
    i, n = pl.program_id(0), pl.num_programs(0)
    slot = jnp.bitwise_and(i, 1)

    def start(blk, sl):
        def issue(grp, carry):
            for n in range(ISSUE_UNROLL):
                rr = grp * ISSUE_UNROLL + n
                tok = tok_ref[blk * GB + rr]
                pltpu.make_async_copy(hn_ref.at[pl.ds(tok, 1)], buf_ref.at[sl, pl.ds(rr, 1)],
                                      sem.at[sl]).start(priority=n % 2)
            return carry

        lax.fori_loop(0, GB // ISSUE_UNROLL, issue, 0)

    @pl.when(i == 0)
    def _():
        start(0, 0)

    @pl.when(i + 1 < n)
    def _():
        start(i + 1, 1 - slot)

    pltpu.make_async_copy(hn_ref.at[pl.ds(0, GB)], buf_ref.at[slot], sem.at[slot]).wait()
    for j in range(buf_ref.shape[2]):
        o_ref[:, j * LANES:(j + 1) * LANES] = buf_ref[slot, :, j, :].astype(o_ref.dtype)


def _moe_gather(slot_tok, hn, *, GB=512):
    P = slot_tok.shape[0]
    D = hn.shape[1] * hn.shape[2]
    return pl.pallas_call(
        functools.partial(_moe_gather_kernel, GB=GB),
        grid_spec=pltpu.PrefetchScalarGridSpec(
            num_scalar_prefetch=1, grid=(P // GB,),
            in_specs=[pl.BlockSpec(memory_space=pl.ANY)],
            out_specs=pl.BlockSpec((GB, D), lambda i, tok: (i, 0)),
            scratch_shapes=[pltpu.VMEM((2, GB) + hn.shape[1:], F32), pltpu.SemaphoreType.DMA((2,))]),
        out_shape=jax.ShapeDtypeStruct((P, D), BF16),
        compiler_params=_cparams(("arbitrary",), 24),
        name="moe_gather",
    )(slot_tok, hn)


def _moe_expert_kernel(ie_ref, ib_ref, ins_ref, x_ref, w1_ref, w3_ref, w2_ref, y_ref, acc_ref, *, SUB):
    w, f, s = pl.program_id(0), pl.program_id(1), pl.program_id(2)
    del ie_ref, ib_ref
    nsub = pl.num_programs(2)
    last_f = f == pl.num_programs(1) - 1
    n = ins_ref[w]

    @pl.when(last_f & (s >= nsub + n) & (n < 0))
    def _():
        y_ref[...] = jnp.zeros_like(y_ref)

    @pl.when((s >= nsub - n) & (n > 0))
    def _():
        x = x_ref[...]
        hid = _silu(_dot(x, w1_ref[0])) * _dot(x, w3_ref[0])
        part = _dot(hid, w2_ref[0])
        rows = pl.ds(pl.multiple_of((s - (nsub - n)) * SUB, SUB), SUB)

        @pl.when(f == 0)
        def _():
            acc_ref[rows, :] = part

        @pl.when(f > 0)
        def _():
            acc_ref[rows, :] += part

        @pl.when(last_f)
        def _():
            y_ref[...] = acc_ref[rows, :]


def _moe_experts(xs, item_e, item_b, item_n, w1, w3, w2, *, SUB, NSUB, tf=512):
    P, D = xs.shape
    E, _, DE = w1.shape
    NI = item_e.shape[0]
    nf = DE // tf

    def sub_blk(w, s, ib, ins):
        cnt = jnp.abs(ins[w])
        return ib[w] + jnp.clip(s - (NSUB - cnt), 0, jnp.maximum(cnt - 1, 0))

    return pl.pallas_call(
        functools.partial(_moe_expert_kernel, SUB=SUB),
        grid_spec=pltpu.PrefetchScalarGridSpec(
            num_scalar_prefetch=3, grid=(NI, nf, NSUB),
            in_specs=[
                pl.BlockSpec((SUB, D), lambda w, f, s, ie, ib, ins: (sub_blk(w, s, ib, ins), 0)),
                pl.BlockSpec((1, D, tf), lambda w, f, s, ie, ib, ins: (ie[w], 0, f)),
                pl.BlockSpec((1, D, tf), lambda w, f, s, ie, ib, ins: (ie[w], 0, f)),
                pl.BlockSpec((1, tf, D), lambda w, f, s, ie, ib, ins: (ie[w], f, 0)),
            ],
            out_specs=pl.BlockSpec(
                (SUB, D), lambda w, f, s, ie, ib, ins: (jnp.where(f == nf - 1, sub_blk(w, s, ib, ins), ib[w]), 0)),
            scratch_shapes=[pltpu.VMEM((NSUB * SUB, D), F32)]),
        out_shape=jax.ShapeDtypeStruct((P, D), F32),
        compiler_params=_cparams(("arbitrary",) * 3, 58),
        name="moe_experts",
    )(item_e, item_b, item_n, xs, w1, w3, w2)


def _moe_combine_kernel(pos_ref, h_ref, g_ref, nw_ref, y_hbm, o_ref, b0_ref, b1_ref, sem, *, tm, tok0):
    base = (tok0 + pl.program_id(0) * tm) * TOP_K

    def issue(grp, carry):
        for n in range(ISSUE_UNROLL):
            rr = grp * ISSUE_UNROLL + n
            for k, buf in enumerate((b0_ref, b1_ref)):
                pltpu.make_async_copy(y_hbm.at[pl.ds(pos_ref[base + TOP_K * rr + k], 1)], buf.at[pl.ds(rr, 1)],
                                      sem).start(priority=k)
        return carry

    lax.fori_loop(0, tm // ISSUE_UNROLL, issue, 0)
    pltpu.make_async_copy(y_hbm.at[pl.ds(0, tm)], b0_ref, sem).wait()
    pltpu.make_async_copy(y_hbm.at[pl.ds(0, tm)], b1_ref, sem).wait()
    g = g_ref[...]
    hf = h_ref[...] + (g[:, 0:1] * b0_ref[...] + g[:, 1:2] * b1_ref[...])
    o_ref[...] = hf * lax.rsqrt(jnp.mean(hf * hf, axis=-1, keepdims=True) + RMS_EPS) * nw_ref[...]


def _moe_combine(pos, h, gates, nw, y, tok0, *, tm=256):
    T, D = h.shape
    tm = min(tm, T)
    return pl.pallas_call(
        functools.partial(_moe_combine_kernel, tm=tm, tok0=tok0),
        grid_spec=pltpu.PrefetchScalarGridSpec(
            num_scalar_prefetch=1, grid=(T // tm,),
            in_specs=[pl.BlockSpec((tm, D), lambda i, p: (i, 0)),
                      pl.BlockSpec((tm, LANES), lambda i, p: (i, 0)),
                      pl.BlockSpec((1, D), lambda i, p: (0, 0)),
                      pl.BlockSpec(memory_space=pl.ANY)],
            out_specs=pl.BlockSpec((tm, D), lambda i, p: (i, 0)),
            scratch_shapes=[pltpu.VMEM((tm,) + y.shape[1:], F32), pltpu.VMEM((tm,) + y.shape[1:], F32),
                            pltpu.SemaphoreType.DMA(())]),
        out_shape=jax.ShapeDtypeStruct((T, D), F32),
        compiler_params=_cparams(("arbitrary",), 32),
        name="moe_combine",
    )(pos, h, gates, nw.reshape(1, D), y)


def _moe_plan(e_ids, *, SUB, NSUB):
    A = e_ids.shape[0]
    E = N_EXPERTS
    P = ((A + E * (SUB - 1) + SUB - 1) // SUB) * SUB
    order = jnp.argsort(e_ids, stable=True).astype(I32)
    counts = jnp.sum((e_ids[:, None] == jnp.arange(E, dtype=I32)[None, :]).astype(I32), axis=0)
    padded = ((counts + SUB - 1) // SUB) * SUB
    start = jnp.cumsum(counts) - counts
    pend = jnp.cumsum(padded)
    pstart = pend - padded
    se = e_ids[order]
    dest_sorted = pstart[se] + jnp.arange(A, dtype=I32) - start[se]
    pos = dest_sorted[jnp.argsort(order).astype(I32)]
    p = jnp.arange(P, dtype=I32)
    pe = jnp.minimum(jnp.sum((p[:, None] >= pend[None, :]).astype(I32), axis=1), E - 1)
    rank = p - pstart[pe]
    src = order[jnp.clip(start[pe] + rank, 0, A - 1)] // TOP_K
    slot_tok = jnp.where(rank < counts[pe], src, 0).astype(I32)
    nblk = padded // SUB
    n_items = (nblk + NSUB - 1) // NSUB
    iend = jnp.cumsum(n_items)
    n_blocks = P // SUB
    NI = E + (n_blocks + NSUB - 1) // NSUB
    w = jnp.arange(NI, dtype=I32)
    total = iend[-1]
    wc = jnp.minimum(w, total - 1)
    ie = jnp.minimum(jnp.sum((wc[:, None] >= iend[None, :]).astype(I32), axis=1), E - 1)
    k = wc - (iend[ie] - n_items[ie])
    ib = pstart[ie] // SUB + k * NSUB
    ins = jnp.minimum(NSUB, nblk[ie] - k * NSUB)
    valid = w < total
    tail0 = jnp.sum(nblk) + (w - total) * NSUB
    tail_n = jnp.clip(n_blocks - tail0, 0, NSUB)
    return (slot_tok, pos, ie.astype(I32), jnp.where(valid, ib, jnp.minimum(tail0, n_blocks - 1)).astype(I32),
            jnp.where(valid, ins, -tail_n).astype(I32))


def _mixer(x2, B, L, sb_fn, mem_k, mem_v, conv_prev, state0, lw, hn_all, row0, *, TL, CP, HB):
    (norm_mix_w, w_a, w_b, w_ab, dn_conv_w, dn_a_log, dn_dt_bias, dn_norm_w,
     w_sb_o, w_dn_o, w_x_o, w_out_bf, norm_ffn_w, w_router, b_router) = lw
    D = x2.shape[1]
    proj, ab = _proj(x2, norm_mix_w, w_a, COL_AB, w_b, w_ab)
    o_sb = sb_fn(proj)
    o_dn, dn_state = _deltanet(proj, ab, conv_prev, state0, dn_conv_w, dn_a_log, dn_dt_bias, dn_norm_w,
                               B, L, TL=TL, CP=CP, HB=HB)
    o_x = _cross_attn(proj, mem_k, mem_v, B, L)
    merged = _merge(o_sb, o_dn, o_x, w_sb_o, w_dn_o, w_x_o, proj, D)
    h, hn, ids, gates = _out_router(x2, merged, w_out_bf, norm_ffn_w, w_router, b_router, hn_all, row0)
    k_new = proj[:, COL_KSB:COL_KSB + SB_W].reshape(B, L, SB_HEADS, SB_HEAD_DIM)
    v_new = proj[:, COL_VSB:COL_VSB + SB_W].reshape(B, L, SB_HEADS, SB_HEAD_DIM)
    K = DN_CONV - 1
    tail = proj.reshape(B, L, proj.shape[1])[:, L - min(L, K):, COL_QDN:COL_ZDN]
    new_conv = jnp.concatenate([conv_prev, tail], axis=1)[:, -K:]
    return h, hn, ids, gates, k_new, v_new, dn_state, new_conv


def kernel(x_prompt, x_sample, cache_sb_k, cache_sb_v, cache_mem_k, cache_mem_v, state_dn, state_dn_conv,
           page_table, mem_prompt, norm_mix_w, w_in, sb_bias, dn_conv_w, dn_a_log, dn_dt_bias, dn_norm_w,
           mem_norm_w, w_mem_k, w_mem_v, w_sb_o, w_dn_o, w_x_o, w_out, norm_ffn_w, w_router_g, b_router_g,
           w_router_e, b_router_e, w_e1, w_e3, w_e2, norm_final_w):
    BP, S, D = x_prompt.shape
    DB, Q, _ = x_sample.shape
    depth = w_in.shape[0]
    n_mem = mem_prompt.shape[1]
    TP, TS = BP * S, DB * Q
    SUB, NSUB = 512, 3
    hp, hs = x_prompt.reshape(TP, D), x_sample.reshape(TS, D)
    outs = {k: [] for k in ("sb_kp", "sb_vp", "mem_kp", "mem_vp", "dn_sp", "dn_cp", "sb_ks", "sb_vs", "dn_ss", "dn_cs")}
    for l in range(depth):
        w_ab = jnp.pad(w_in[l][:, COL_AB:COL_AB + 2 * DN_HEADS], ((0, 0), (0, LANES - 2 * DN_HEADS)))
        w_a = w_in[l][:, :COL_AB].astype(BF16)
        w_b = w_in[l][:, COL_AB + 2 * DN_HEADS:].astype(BF16)
        n_r = N_GROUPS + N_EXPERTS
        w_router = jnp.pad(jnp.concatenate([w_router_g[l], w_router_e[l]], axis=1), ((0, 0), (0, LANES - n_r)))
        b_router = jnp.pad(jnp.concatenate([b_router_g[l], b_router_e[l]]), (0, LANES - n_r)).reshape(1, LANES)
        lw = (norm_mix_w[l], w_a, w_b, w_ab, dn_conv_w[l], dn_a_log[l], dn_dt_bias[l], dn_norm_w[l],
              w_sb_o[l], w_dn_o[l], w_x_o[l], w_out[l].astype(BF16), norm_ffn_w[l], w_router, b_router.astype(F32))
        mkv = _proj(mem_prompt.reshape(BP * n_mem, D), mem_norm_w[l], w_mem_k[l], X_W, w_mem_v[l], tm=512, tn=512)
        mk, mv = mkv[:, :X_W].reshape(BP, n_mem, X_W), mkv[:, X_W:].reshape(BP, n_mem, X_W)
        conv0 = jnp.zeros((BP, DN_CONV - 1, 3 * DN_W), F32)
        s0 = jnp.zeros((BP, DN_HEADS, DN_KDIM, DN_VDIM), F32)
        sbp = functools.partial(_sb_prompt, bias=sb_bias[l].astype(F32), B=BP, S=S)
        hp, hn_all, idp, gp, k_new, v_new, s_new, c_new = _mixer(hp, BP, S, sbp, mk, mv, conv0, s0, lw, TP + TS, 0,
                                                                 TL=256, CP=256, HB=4)
        outs["sb_kp"].append(k_new); outs["sb_vp"].append(v_new)
        outs["mem_kp"].append(mk.reshape(BP, n_mem, X_HEADS, X_HEAD_DIM))
        outs["mem_vp"].append(mv.reshape(BP, n_mem, X_HEADS, X_HEAD_DIM))
        outs["dn_sp"].append(s_new); outs["dn_cp"].append(c_new)
        sbs = functools.partial(_sb_sample, bias=sb_bias[l].astype(F32), cache_k=cache_sb_k, cache_v=cache_sb_v,
                                page_table=page_table, layer=l, DB=DB, Q=Q)
        hs, hn_all, ids_, gs, k_new, v_new, s_new, c_new = _mixer(
            hs, DB, Q, sbs, cache_mem_k[l].reshape(DB, n_mem, X_W), cache_mem_v[l].reshape(DB, n_mem, X_W),
            state_dn_conv[l], state_dn[l], lw, hn_all, TP, TL=Q, CP=LANES, HB=DN_HEADS)
        outs["sb_ks"].append(k_new); outs["sb_vs"].append(v_new)
        outs["dn_ss"].append(s_new); outs["dn_cs"].append(c_new)
        e_ids = jnp.concatenate([idp[:, :TOP_K], ids_[:, :TOP_K]], axis=0).reshape(-1)
        slot_tok, pos, ie, ib, ins = _moe_plan(e_ids, SUB=SUB, NSUB=NSUB)
        xs = _moe_gather(slot_tok, hn_all)
        y = _moe_experts(xs, ie, ib, ins, w_e1[l], w_e3[l], w_e2[l], SUB=SUB, NSUB=NSUB)
        last = l == depth - 1
        nfw = norm_final_w if last else None
        assert last, "multi-layer stacking needs the un-normalised residual; only the final layer applies norm_final"
        hp = _moe_combine(pos, hp, gp, nfw, y, 0)
        hs = _moe_combine(pos, hs, gs, nfw, y, TP)
    stack = lambda k: jnp.stack(outs[k])
    return (hp.reshape(BP, S, D), hs.reshape(DB, Q, D), stack("sb_kp"), stack("sb_vp"), stack("mem_kp"),
            stack("mem_vp"), stack("dn_sp"), stack("dn_cp"), stack("sb_ks"), stack("sb_vs"), stack("dn_ss"),
            stack("dn_cs"))
```

```python
import functools

import jax
import jax.numpy as jnp
from jax import lax
from jax.experimental import pallas as pl
from jax.experimental.pallas import tpu as pltpu

F32, BF16, I32 = jnp.float32, jnp.bfloat16, jnp.int32

RMS_EPS = 1e-6
L2_EPS = 1e-6
SB_HEADS, SB_HEAD_DIM = 8, 128
DN_HEADS, DN_KDIM, DN_VDIM, DN_CONV = 8, 128, 128, 4
X_HEADS, X_HEAD_DIM = 4, 256
N_GROUPS, EXPERTS_PER_GROUP, TOP_K = 4, 8, 2
N_EXPERTS = N_GROUPS * EXPERTS_PER_GROUP
PAGE_SIZE = 128

LANES = 128
ISSUE_UNROLL = 8
MIB = 1024 * 1024
LOG2E = 1.4426950408889634

SB_W = SB_HEADS * SB_HEAD_DIM
DN_W = DN_HEADS * DN_KDIM
X_W = X_HEADS * X_HEAD_DIM
COL_QSB, COL_KSB, COL_VSB = 0, SB_W, 2 * SB_W
COL_QDN = 3 * SB_W
COL_KDN, COL_VDN, COL_ZDN = COL_QDN + DN_W, COL_QDN + 2 * DN_W, COL_QDN + 3 * DN_W
COL_AB = COL_ZDN + DN_W
COL_QX = COL_AB
NT = (((1,), (1,)), ((), ()))
TN = (((0,), (0,)), ((), ()))


def _cparams(sem, vmem_mib):
    return pltpu.CompilerParams(dimension_semantics=sem, vmem_limit_bytes=int(vmem_mib * MIB))


def _dot(a, b):
    return jnp.dot(a.astype(BF16), b.astype(BF16), preferred_element_type=F32)


def _dot_g(a, b, dims):
    return lax.dot_general(a.astype(BF16), b.astype(BF16), dims, preferred_element_type=F32)


def _split2(a):
    hi = a.astype(BF16)
    lo = (a - hi.astype(F32)).astype(BF16)
    return hi, lo


def _dot3(a, b):
    ah, al = _split2(a)
    bh, bl = _split2(b)
    d = functools.partial(jnp.dot, preferred_element_type=F32)
    return d(ah, bh) + (d(ah, bl) + d(al, bh))


def _softplus(z):
    return jnp.maximum(z, 0.0) + jnp.log(1.0 + jnp.exp(-jnp.abs(z)))


def _sigmoid(z):
    return 1.0 / (1.0 + jnp.exp(-z))


def _silu(z):
    return z * _sigmoid(z)


def _iota2(shape):
    return lax.broadcasted_iota(I32, shape, 0), lax.broadcasted_iota(I32, shape, 1)


def _div_pow2(x, p):
    assert p & (p - 1) == 0, p
    return lax.shift_right_logical(x, p.bit_length() - 1)


def _proj_kernel(*refs, na, has_extra):
    if has_extra:
        x_ref, nw_ref, wa_ref, wb_ref, we_ref, o_ref, e_ref, xn_ref = refs
    else:
        x_ref, nw_ref, wa_ref, wb_ref, o_ref, xn_ref = refs
    j = pl.program_id(1)

    @pl.when(j == 0)
    def _():
        x = x_ref[...]
        ms = jnp.mean(x * x, axis=-1, keepdims=True)
        xn = (x * lax.rsqrt(ms + RMS_EPS) * nw_ref[...]).astype(BF16)
        xn_ref[...] = xn
        if has_extra:
            e_ref[...] = jnp.dot(xn, we_ref[...].astype(BF16), preferred_element_type=F32)

    @pl.when(j < na)
    def _():
        o_ref[...] = jnp.dot(xn_ref[...], wa_ref[...].astype(BF16), preferred_element_type=F32)

    @pl.when(j >= na)
    def _():
        o_ref[...] = jnp.dot(xn_ref[...], wb_ref[...].astype(BF16), preferred_element_type=F32)


def _proj(x, nw, wa, na_cols, wb, w_extra=None, *, tm=1024, tn=1024):
    T, D = x.shape
    tm = min(tm, T)
    assert T % tm == 0 and na_cols % tn == 0 and wb.shape[1] % tn == 0
    na, nb = na_cols // tn, wb.shape[1] // tn
    has_extra = w_extra is not None
    in_specs = [
        pl.BlockSpec((tm, D), lambda i, j: (i, 0)),
        pl.BlockSpec((1, D), lambda i, j: (0, 0)),
        pl.BlockSpec((D, tn), lambda i, j: (0, jnp.minimum(j, na - 1))),
        pl.BlockSpec((D, tn), lambda i, j: (0, jnp.maximum(j - na, 0))),
    ]
    out_shape = [jax.ShapeDtypeStruct((T, (na + nb) * tn), F32)]
    out_specs = [pl.BlockSpec((tm, tn), lambda i, j: (i, j))]
    args = [x, nw.reshape(1, D), wa, wb]
    if has_extra:
        in_specs.append(pl.BlockSpec((D, LANES), lambda i, j: (0, 0)))
        out_shape.append(jax.ShapeDtypeStruct((T, LANES), F32))
        out_specs.append(pl.BlockSpec((tm, LANES), lambda i, j: (i, 0)))
        args.append(w_extra)
    wbytes = 2 * D * tn * (wa.dtype.itemsize + wb.dtype.itemsize)
    vmem = (2 * tm * D * 4 + tm * D * 2 + wbytes + 2 * tm * tn * 4 + 4 * MIB) / MIB + 6
    res = pl.pallas_call(
        functools.partial(_proj_kernel, na=na, has_extra=has_extra),
        grid=(T // tm, na + nb),
        in_specs=in_specs, out_specs=out_specs, out_shape=out_shape,
        scratch_shapes=[pltpu.VMEM((tm, D), BF16)],
        compiler_params=_cparams(("arbitrary", "arbitrary"), vmem),
        name="proj",
    )(*args)
    return res if has_extra else res[0]


def _sb_weights(z2s, masks, u2, acc):
    sp = [jnp.maximum(z, 0.0) + jnp.log(1.0 + jnp.exp2(jnp.minimum(z, -z))) * LOG2E for z in z2s]
    sp = [s if m is None else jnp.where(m, s, 0.0) for s, m in zip(sp, masks)]
    cum = [jnp.dot(s.astype(BF16), u2, preferred_element_type=F32) for s in sp]
    es = [z - c for z, c in zip(z2s, cum)]
    ws = []
    for e, c, m in zip(es, cum, masks):
        w = jnp.exp2(e - acc)
        ws.append(w if m is None else jnp.where(m, w, 0.0))
        acc = acc + c[:, 0:1]
    return ws, acc


def _later_key_matrix(n):
    r, c = _iota2((n, n))
    return jnp.where(r >= c, 1.0, 0.0).astype(BF16)


def _sb_prompt_kernel(bias_ref, q_ref, k_ref, v_ref, o_ref, kb_ref, vb_ref, *, tq, scale, unroll):
    h, i = pl.program_id(1), pl.program_id(2)

    @pl.when(i == 0)
    def _():
        kb_ref[...] = k_ref[...].astype(BF16)
        vb_ref[...] = v_ref[...].astype(BF16)

    bias2 = bias_ref[h] * LOG2E
    q = (q_ref[...] * (scale * LOG2E)).astype(BF16)
    u2 = _later_key_matrix(tq)
    r, c = _iota2((tq, tq))
    strict = c < r

    def run(blocks, masks, o, acc):
        sls = [pl.ds(pl.multiple_of(j * tq, tq), tq) for j in blocks]
        z2s = [lax.dot_general(q, kb_ref[sl, :], NT, preferred_element_type=F32) + bias2 for sl in sls]
        ws, acc = _sb_weights(z2s, masks, u2, acc)
        pvs = [jnp.dot(w.astype(BF16), vb_ref[sl, :], preferred_element_type=F32) for w, sl in zip(ws, sls)]
        while len(pvs) > 1:
            pvs = [a + b for a, b in zip(pvs[::2], pvs[1::2])] + ([pvs[-1]] if len(pvs) % 2 else [])
        return o + pvs[0], acc

    rem = jnp.bitwise_and(i, unroll - 1)
    zero = (jnp.zeros(o_ref.shape, F32), jnp.zeros((tq, 1), F32))
    head = [functools.partial(lambda n: run([i - t for t in range(n + 1)], [strict] + [None] * n, *zero), n)
            for n in range(unroll)]
    carry = lax.switch(rem, head)
    top = i - 1 - rem

    def body(t, cr):
        j0 = top - t * unroll
        return run([j0 - n for n in range(unroll)], [None] * unroll, *cr)

    o, _ = lax.fori_loop(0, _div_pow2(i, unroll), body, carry)
    o_ref[...] = o.astype(o_ref.dtype)


def _sb_prompt(proj, bias, B, S, *, tq=256, unroll=4):
    H, d = SB_HEADS, SB_HEAD_DIM
    nq = S // tq
    return pl.pallas_call(
        functools.partial(_sb_prompt_kernel, tq=tq, scale=d ** -0.5, unroll=unroll),
        grid=(B, H, nq),
        in_specs=[
            pl.BlockSpec(memory_space=pltpu.SMEM),
            pl.BlockSpec((tq, d), lambda b, h, i: (b * nq + i, COL_QSB // d + h)),
            pl.BlockSpec((S, d), lambda b, h, i: (b, COL_KSB // d + h)),
            pl.BlockSpec((S, d), lambda b, h, i: (b, COL_VSB // d + h)),
        ],
        out_specs=pl.BlockSpec((tq, d), lambda b, h, i: (b * nq + i, h)),
        out_shape=jax.ShapeDtypeStruct((B * S, H * d), BF16),
        scratch_shapes=[pltpu.VMEM((S, d), BF16), pltpu.VMEM((S, d), BF16)],
        compiler_params=_cparams(("arbitrary",) * 3, 4 * S * d * 4 / MIB + 2 * S * d * 2 / MIB + 16),
        name="sb_prompt",
    )(bias, proj, proj, proj)


def _sb_sample_kernel(pt_ref, bias_ref, q_ref, kn_ref, vn_ref, *rest, G, scale):
    del pt_ref
    k_refs, v_refs = rest[:G], rest[G:2 * G]
    o_ref, acc_ref, out_ref = rest[2 * G:]
    H, d, P = SB_HEADS, SB_HEAD_DIM, PAGE_SIZE
    Q = q_ref.shape[0]
    HQ = H * Q
    g = pl.program_id(1)
    u2 = _later_key_matrix(P)
    qs = [(q_ref[:, h * d:(h + 1) * d] * (scale * LOG2E)).astype(BF16) for h in range(H)]

    def attend(tiles, acc, outs):
        z2s = [jnp.concatenate(
            [lax.dot_general(qs[h], kfn(h).astype(BF16), NT, preferred_element_type=F32) for h in range(H)],
            axis=0) + bias_ref[...] for kfn, _, _ in tiles]
        ws, acc = _sb_weights(z2s, [m for _, _, m in tiles], u2, acc)
        for w, (_, vfn, _) in zip(ws, tiles):
            outs = [outs[h] + jnp.dot(w[h * Q:(h + 1) * Q].astype(BF16), vfn(h).astype(BF16),
                                      preferred_element_type=F32) for h in range(H)]
        return acc, outs

    def finish(acc, outs):
        acc_ref[...] = acc
        out_ref[...] = jnp.concatenate(outs, axis=0)

    @pl.when(g == 0)
    def _():
        pad = jnp.zeros((P - Q, d), F32)
        rq, ck = _iota2((HQ, P))
        new = (lambda h: jnp.concatenate([kn_ref[:, h * d:(h + 1) * d], pad], axis=0),
               lambda h: jnp.concatenate([vn_ref[:, h * d:(h + 1) * d], pad], axis=0),
               ck < jnp.bitwise_and(rq, Q - 1))
        finish(*attend([new], jnp.zeros((HQ, 1), F32), [jnp.zeros((Q, d), F32)] * H))

    tiles = [(lambda h, t=t: k_refs[t][0, pl.ds(h, P, stride=H), :],
              lambda h, t=t: v_refs[t][0, pl.ds(h, P, stride=H), :], None) for t in range(G)]
    finish(*attend(tiles, acc_ref[...], [out_ref[h * Q:(h + 1) * Q, :] for h in range(H)]))

    @pl.when(g == pl.num_programs(1) - 1)
    def _():
        for h in range(H):
            o_ref[:, h * d:(h + 1) * d] = out_ref[h * Q:(h + 1) * Q, :].astype(o_ref.dtype)


def _sb_sample(proj, bias, cache_k, cache_v, page_table, layer, DB, Q, *, G=8):
    H, d = SB_HEADS, SB_HEAD_DIM
    W = H * d
    depth, n_pool = cache_k.shape[:2]
    NP = page_table.shape[1]
    ck = cache_k.reshape(depth * n_pool, PAGE_SIZE * H, d)
    cv = cache_v.reshape(depth * n_pool, PAGE_SIZE * H, d)
    base = layer * n_pool
    bias_rows = jnp.broadcast_to(jnp.repeat(bias * LOG2E, Q)[:, None], (H * Q, PAGE_SIZE)).astype(F32)

    def page_spec(t):
        return pl.BlockSpec((1, PAGE_SIZE * H, d), lambda b, g, pt: (base + pt[b, NP - 1 - (g * G + t)], 0, 0))

    in_specs = [
        pl.BlockSpec((H * Q, PAGE_SIZE), lambda b, g, pt: (0, 0)),
        pl.BlockSpec((Q, W), lambda b, g, pt: (b, COL_QSB // W)),
        pl.BlockSpec((Q, W), lambda b, g, pt: (b, COL_KSB // W)),
        pl.BlockSpec((Q, W), lambda b, g, pt: (b, COL_VSB // W)),
    ] + [page_spec(t) for t in range(G)] * 2
    return pl.pallas_call(
        functools.partial(_sb_sample_kernel, G=G, scale=d ** -0.5),
        grid_spec=pltpu.PrefetchScalarGridSpec(
            num_scalar_prefetch=1, grid=(DB, NP // G),
            in_specs=in_specs,
            out_specs=pl.BlockSpec((Q, W), lambda b, g, pt: (b, 0)),
            scratch_shapes=[pltpu.VMEM((H * Q, 1), F32), pltpu.VMEM((H * Q, d), F32)]),
        out_shape=jax.ShapeDtypeStruct((DB * Q, W), BF16),
        compiler_params=_cparams(("arbitrary", "arbitrary"), 4 * G * PAGE_SIZE * W * 4 / MIB + 12),
        name="sb_sample",
    )(page_table, bias_rows, proj, proj, proj, *([ck] * G), *([cv] * G))


def _unit_lower_inverse_m1(lms, n):
    r, c = _iota2((n, n))
    same = lambda s: _div_pow2(r, s) == _div_pow2(c, s)
    each = lambda f, *ls: [f(*a) for a in zip(*ls)]
    l16 = [jnp.where(same(16), lm, 0.0) for lm in lms]
    l2 = each(_dot, l16, l16)
    l4 = each(_dot, l2, l2)
    l8 = each(_dot, l4, l4)
    xr = [-a for a in l16]
    for p in (l2, l4, l8):
        xr = each(lambda x, q: x + q + _dot(x, q), xr, p)
    size = 32
    while size <= n:
        block = jnp.where(same(size), jnp.where(same(size // 2), 0.0, 1.0), 0.0)
        y = each(lambda x, lm: lm * block + _dot(x, lm * block), xr, lms)
        xr = each(lambda x, yy: x - yy - _dot(yy, x), xr, y)
        size *= 2
    return xr


def _dot2(a, b):
    ah, al = _split2(a)
    bh = b.astype(BF16)
    return jnp.dot(ah, bh, preferred_element_type=F32) + jnp.dot(al, bh, preferred_element_type=F32)


def _dn_kernel(q_ref, k_ref, v_ref, z_ref, ab_ref, cpq_ref, cpk_ref, cpv_ref, cwq_ref, cwk_ref, cwv_ref,
               s0_ref, alog_ref, dtb_ref, nw_ref, o_ref, s_out_ref, s_ref, prev_ref, seq_ref, *, TL, CP, HB):
    hg, i = pl.program_id(1), pl.program_id(2)
    K = DN_CONV - 1
    dk = DN_KDIM

    @pl.when(i == 0)
    def _():
        s_ref[...] = s0_ref[0]
        prev_ref[...] = jnp.zeros_like(prev_ref)
        for n, cp in enumerate((cpq_ref, cpk_ref, cpv_ref)):
            prev_ref[n, 8 - K:8, :] = cp[0]

    def conv(n, x_ref, cw_ref):
        seq_ref[0:8, :] = prev_ref[n]
        seq_ref[8:8 + TL, :] = x_ref[...]
        cw = cw_ref[...]
        acc = seq_ref[8:8 + TL, :] * cw[K:K + 1, :]
        for s in range(1, DN_CONV):
            acc = acc + seq_ref[8 - s:8 - s + TL, :] * cw[K - s:K - s + 1, :]
        prev_ref[n] = seq_ref[TL:TL + 8, :]
        y = _silu(acc)
        if CP > TL:
            y = jnp.concatenate([y, jnp.zeros((CP - TL, y.shape[1]), F32)], axis=0)
        return y

    qc_all = conv(0, q_ref, cwq_ref)
    kc_all = conv(1, k_ref, cwk_ref)
    vc_all = conv(2, v_ref, cwv_ref)
    ab = ab_ref[...]
    lane = lax.broadcasted_iota(I32, ab.shape, 1)
    r, c = _iota2((CP, CP))
    causal = r >= c
    tri = jnp.where(causal, 1.0, 0.0).astype(BF16)
    d = functools.partial(jnp.dot, preferred_element_type=F32)

    heads = range(HB)
    each = lambda f, *ls: [f(*a) for a in zip(*ls)]
    sls = [slice(hh * dk, (hh + 1) * dk) for hh in heads]
    l2n = lambda t: t * lax.rsqrt(jnp.sum(t * t, axis=-1, keepdims=True) + L2_EPS)
    qn = [l2n(qc_all[:, sl]) * (DN_KDIM ** -0.5) for sl in sls]
    kn = [l2n(kc_all[:, sl]) for sl in sls]
    vc = [vc_all[:, sl] for sl in sls]
    pick = lambda col: jnp.sum(jnp.where(lane == col, ab, 0.0), axis=-1, keepdims=True)
    pad0 = (lambda t: jnp.concatenate([t, jnp.zeros((CP - TL, 1), F32)], axis=0)) if CP > TL else (lambda t: t)
    beta = [pad0(_sigmoid(pick(DN_HEADS + hg * HB + hh))) for hh in heads]
    gl = [pad0(-jnp.exp(alog_ref[hg * HB + hh][:, 0:1]) * _softplus(pick(hg * HB + hh) + dtb_ref[hg * HB + hh][:, 0:1]))
          for hh in heads]

    def cumulative(g):
        g1 = jnp.broadcast_to(g, (CP, LANES))
        h1 = g1.astype(BF16)
        r1 = g1 - h1.astype(F32)
        h2 = r1.astype(BF16)
        h3 = (r1 - h2.astype(F32)).astype(BF16)
        return d(tri, h1) + (d(tri, h2) + d(tri, h3))

    gcum_b = [cumulative(g) for g in gl]
    gc = [t[:, 0:1] for t in gcum_b]
    gr = [jnp.transpose(t)[0:1, :] for t in gcum_b]
    decay = each(lambda a, b: jnp.where(causal, jnp.exp(jnp.where(causal, a - b, 0.0)), 0.0), gc, gr)
    kb = each(lambda k, b: k * b, kn, beta)
    lm = each(lambda a, b, dc: jnp.where(r > c, _dot_g(a, b, NT) * dc, 0.0), kb, kn, decay)
    qk = each(lambda a, b, dc: _dot_g(a, b, NT) * dc, qn, kn, decay)
    xr = _unit_lower_inverse_m1(lm, CP)
    eg = [jnp.exp(t) for t in gc]
    solve = lambda x, rhs: rhs + _dot2(x, rhs)
    u = each(lambda x, v, b: solve(x, v * b), xr, vc, beta)
    w = each(lambda x, k, e: solve(x, k * e), xr, kb, eg)
    s = [s_ref[hh] for hh in heads]
    v_new = each(lambda uu, ww, ss: uu - _dot(ww, ss), u, w, s)
    o = each(lambda q, e, ss, a, vn: _dot(q * e, ss) + _dot(a, vn), qn, eg, s, qk, v_new)
    g_last = [t[CP - 1:CP, :] for t in gc]
    s_new = each(lambda ss, gz, k, g, vn: ss * jnp.exp(gz) + _dot_g(k * jnp.exp(gz - g), vn, TN),
                 s, g_last, kn, gc, v_new)
    for hh in heads:
        s_ref[hh] = s_new[hh]
        oh = o[hh][:TL]
        oh = oh * lax.rsqrt(jnp.mean(oh * oh, axis=-1, keepdims=True) + RMS_EPS) * nw_ref[...]
        o_ref[:, sls[hh]] = (oh * _silu(z_ref[:, sls[hh]])).astype(o_ref.dtype)

    @pl.when(i == pl.num_programs(2) - 1)
    def _():
        s_out_ref[0] = s_ref[...]


def _deltanet(proj, ab, conv_prev, state0, conv_w, a_log, dt_bias, norm_w, B, L, *, TL, CP, HB):
    H, dk = DN_HEADS, DN_KDIM
    nl = L // TL
    wb = HB * dk
    cq, ck, cv, cz = COL_QDN // wb, COL_KDN // wb, COL_VDN // wb, COL_ZDN // wb
    row = lambda off: pl.BlockSpec((TL, wb), lambda b, h, i: (b * nl + i, off + h))
    cprev = lambda off: pl.BlockSpec((1, DN_CONV - 1, wb), lambda b, h, i: (b, 0, off + h))
    cwt = lambda off: pl.BlockSpec((DN_CONV, wb), lambda b, h, i: (0, off + h))
    per_head = pl.BlockSpec((H, 1, LANES), lambda b, h, i: (0, 0, 0))
    st = pl.BlockSpec((1, HB, dk, DN_VDIM), lambda b, h, i: (b, h, 0, 0))
    alog_b = jnp.broadcast_to(a_log.astype(F32)[:, None, None], (H, 1, LANES))
    dtb_b = jnp.broadcast_to(dt_bias.astype(F32)[:, None, None], (H, 1, LANES))
    o, s = pl.pallas_call(
        functools.partial(_dn_kernel, TL=TL, CP=CP, HB=HB),
        grid=(B, H // HB, nl),
        in_specs=[row(cq), row(ck), row(cv), row(cz),
                  pl.BlockSpec((TL, LANES), lambda b, h, i: (b * nl + i, 0)),
                  cprev(0), cprev(H // HB), cprev(2 * H // HB), cwt(0), cwt(H // HB), cwt(2 * H // HB),
                  st, per_head, per_head,
                  pl.BlockSpec((1, DN_VDIM), lambda b, h, i: (0, 0))],
        out_specs=[pl.BlockSpec((TL, wb), lambda b, h, i: (b * nl + i, h)), st],
        out_shape=[jax.ShapeDtypeStruct((B * L, H * DN_VDIM), BF16),
                   jax.ShapeDtypeStruct((B, H, dk, DN_VDIM), F32)],
        scratch_shapes=[pltpu.VMEM((HB, dk, DN_VDIM), F32), pltpu.VMEM((3, 8, wb), F32),
                        pltpu.VMEM((TL + 8, wb), F32)],
        compiler_params=_cparams(("arbitrary",) * 3, 48),
        name="deltanet",
    )(proj, proj, proj, proj, ab, conv_prev, conv_prev, conv_prev, conv_w, conv_w, conv_w,
      state0, alog_b, dtb_b, norm_w.reshape(1, DN_VDIM).astype(F32))
    return o, s


def _xattn_kernel(q_ref, mk_ref, mv_ref, o_ref):
    dx = X_HEAD_DIM
    for h in range(X_HEADS):
        sl = slice(h * dx, (h + 1) * dx)
        s = _dot_g(q_ref[:, sl], mk_ref[0][:, sl], NT) * (dx ** -0.5)
        e = jnp.exp(s - jnp.max(s, axis=-1, keepdims=True))
        p = e / jnp.sum(e, axis=-1, keepdims=True)
        o_ref[:, sl] = _dot(p, mv_ref[0][:, sl]).astype(o_ref.dtype)


def _cross_attn(proj, mem_k, mem_v, B, L, *, tm=512):
    tm = min(tm, L)
    nl = L // tm
    mem_blk = pl.BlockSpec((1,) + mem_k.shape[1:], lambda b, i: (b, 0, 0))
    return pl.pallas_call(
        _xattn_kernel,
        grid=(B, nl),
        in_specs=[pl.BlockSpec((tm, X_W), lambda b, i: (b * nl + i, COL_QX // X_W)), mem_blk, mem_blk],
        out_specs=pl.BlockSpec((tm, X_W), lambda b, i: (b * nl + i, 0)),
        out_shape=jax.ShapeDtypeStruct((B * L, X_W), BF16),
        compiler_params=_cparams(("arbitrary", "arbitrary"), 32),
        name="cross_attn",
    )(proj, mem_k, mem_v)


def _merge_kernel(osb_ref, odn_ref, ox_ref, wsb_ref, wdn_ref, wx_ref, gsb_ref, gdn_ref, gx_ref, o_ref):
    acc = _sigmoid(gsb_ref[...]) * _dot(osb_ref[...], wsb_ref[...])
    acc = acc + _sigmoid(gdn_ref[...]) * _dot(odn_ref[...], wdn_ref[...])
    acc = acc + _sigmoid(gx_ref[...]) * _dot(ox_ref[...], wx_ref[...])
    o_ref[...] = acc.astype(o_ref.dtype)


def _merge(o_sb, o_dn, o_x, w_sb_o, w_dn_o, w_x_o, proj, D, *, tm=1024, tn=512):
    T = o_sb.shape[0]
    tm = min(tm, T)
    assert T % tm == 0 and D % tn == 0
    g0 = (COL_QX + X_W) // tn
    act = lambda w: pl.BlockSpec((tm, w), lambda i, j: (i, 0))
    wt = lambda k: pl.BlockSpec((k, tn), lambda i, j: (0, j))
    gate = lambda n: pl.BlockSpec((tm, tn), lambda i, j: (i, g0 + n * (D // tn) + j))
    return pl.pallas_call(
        _merge_kernel,
        grid=(T // tm, D // tn),
        in_specs=[act(SB_W), act(DN_HEADS * DN_VDIM), act(X_W), wt(SB_W), wt(DN_HEADS * DN_VDIM), wt(X_W),
                  gate(0), gate(1), gate(2)],
        out_specs=pl.BlockSpec((tm, tn), lambda i, j: (i, j)),
        out_shape=jax.ShapeDtypeStruct((T, D), BF16),
        compiler_params=_cparams(("arbitrary", "arbitrary"), 48),
        name="merge",
    )(o_sb, o_dn, o_x, w_sb_o, w_dn_o, w_x_o, proj, proj, proj)


def _out_router_kernel(x_ref, m_ref, wo_ref, nw_ref, wr_ref, br_ref, *rest, n_main):
    h_ref, hn_ref, ids_ref, gates_ref = rest[-4:]

    @pl.when(pl.program_id(0) >= n_main)
    def _():
        hn_ref[...] = jnp.zeros_like(hn_ref)

    @pl.when(pl.program_id(0) < n_main)
    def _():
        _out_router_body(x_ref, m_ref, wo_ref, nw_ref, wr_ref, br_ref, h_ref, hn_ref, ids_ref, gates_ref)


def _out_router_body(x_ref, m_ref, wo_ref, nw_ref, wr_ref, br_ref, h_ref, hn_ref, ids_ref, gates_ref):
    hres = x_ref[...] + jnp.dot(m_ref[...], wo_ref[...], preferred_element_type=F32)
    h_ref[...] = hres
    hn = hres * lax.rsqrt(jnp.mean(hres * hres, axis=-1, keepdims=True) + RMS_EPS) * nw_ref[...]
    hn_ref[...] = hn
    logits = _dot3(hn, wr_ref[...]) + br_ref[...]
    lane_i = lax.broadcasted_iota(I32, logits.shape, 1)
    lane = lane_i.astype(F32)
    ninf = -jnp.inf
    first = lambda hit: jnp.min(jnp.where(hit, lane, float(LANES)), axis=-1, keepdims=True)
    gl = jnp.where(lane < N_GROUPS, logits, ninf)
    gmax = jnp.max(gl, axis=-1, keepdims=True)
    gidx = first(gl == gmax)
    g_p = 1.0 / jnp.sum(jnp.where(lane < N_GROUPS, jnp.exp(gl - gmax), 0.0), axis=-1, keepdims=True)
    lo = N_GROUPS + EXPERTS_PER_GROUP * gidx
    el = jnp.where(lane >= lo, jnp.where(lane < lo + EXPERTS_PER_GROUP, logits, ninf), ninf)
    m1 = jnp.max(el, axis=-1, keepdims=True)
    i1 = first(el == m1)
    el2 = jnp.where(lane == i1, ninf, el)
    m2 = jnp.max(el2, axis=-1, keepdims=True)
    i2 = first(el2 == m2)
    e2 = jnp.exp(m2 - m1)
    gate1 = g_p / (1.0 + e2)
    ids = jnp.where(lane_i == 0, i1 - N_GROUPS, jnp.where(lane_i == 1, i2 - N_GROUPS, 0.0))
    ids_ref[...] = ids.astype(I32)
    gates_ref[...] = jnp.where(lane_i == 0, gate1, jnp.where(lane_i == 1, gate1 * e2, 0.0))


def _out_router(x, merged, w_out_bf, nw, w_router, b_router, hn_all, row0, *, tm=512):
    T, D = x.shape
    tm = min(tm, T)
    assert T % tm == 0 and row0 % tm == 0
    n_main = T // tm
    rowf = pl.BlockSpec((tm, D), lambda i: (jnp.minimum(i, n_main - 1), 0))
    const = lambda s: pl.BlockSpec(s, lambda i: (0, 0))
    narrow = pl.BlockSpec((tm, LANES), lambda i: (jnp.minimum(i, n_main - 1), 0))
    in_specs = [rowf, rowf, const((D, D)), const((1, D)), const((D, LANES)), const((1, LANES))]
    args = [x, merged, w_out_bf, nw.reshape(1, D), w_router, b_router]
    if isinstance(hn_all, int):
        assert row0 == 0
        t_all, aliases = pl.cdiv(hn_all, tm) * tm, {}
    else:
        t_all, aliases = hn_all.shape[0], {len(args): 1}
        in_specs.append(pl.BlockSpec(memory_space=pl.ANY))
        args.append(hn_all)
        assert t_all % tm == 0 and row0 + T <= t_all
    n_steps = n_main if aliases else t_all // tm
    return pl.pallas_call(
        functools.partial(_out_router_kernel, n_main=n_main),
        grid=(n_steps,),
        in_specs=in_specs,
        out_specs=[rowf, pl.BlockSpec((tm, D), lambda i: (row0 // tm + i, 0)), narrow, narrow],
        out_shape=[jax.ShapeDtypeStruct((T, D), F32), jax.ShapeDtypeStruct((t_all, D), F32),
                   jax.ShapeDtypeStruct((T, LANES), I32), jax.ShapeDtypeStruct((T, LANES), F32)],
        input_output_aliases=aliases,
        compiler_params=_cparams(("arbitrary",), 56),
        name="out_router",
    )(*args)


def _moe_gather_kernel(tok_ref, hn_ref, o_ref, buf_ref, sem, *, GB):
    i, n = pl.program_id(0), pl.num_programs(0)
    slot = jnp.bitwise_and(i, 1)

    def start(blk, sl):
        def issue(grp, carry):
            for n in range(ISSUE_UNROLL):
                rr = grp * ISSUE_UNROLL + n
                tok = tok_ref[blk * GB + rr]
                pltpu.make_async_copy(hn_ref.at[pl.ds(tok, 1)], buf_ref.at[sl, pl.ds(rr, 1)],
                                      sem.at[sl]).start(priority=n % 2)
            return carry

        lax.fori_loop(0, GB // ISSUE_UNROLL, issue, 0)

    @pl.when(i == 0)
    def _():
        start(0, 0)

    @pl.when(i + 1 < n)
    def _():
        start(i + 1, 1 - slot)

    pltpu.make_async_copy(hn_ref.at[pl.ds(0, GB)], buf_ref.at[slot], sem.at[slot]).wait()
    o_ref[...] = buf_ref[slot].astype(o_ref.dtype)


def _moe_gather(slot_tok, hn, *, GB):
    P = slot_tok.shape[0]
    D = hn.shape[1]
    return pl.pallas_call(
        functools.partial(_moe_gather_kernel, GB=GB),
        grid_spec=pltpu.PrefetchScalarGridSpec(
            num_scalar_prefetch=1, grid=(P // GB,),
            in_specs=[pl.BlockSpec(memory_space=pl.ANY)],
            out_specs=pl.BlockSpec((GB, D), lambda i, tok: (i, 0)),
            scratch_shapes=[pltpu.VMEM((2, GB) + hn.shape[1:], F32), pltpu.SemaphoreType.DMA((2,))]),
        out_shape=jax.ShapeDtypeStruct((P, D), BF16),
        compiler_params=_cparams(("arbitrary",), 24),
        name="moe_gather",
    )(slot_tok, hn)


def _moe_expert_kernel(ie_ref, ib_ref, ins_ref, x_ref, w1_ref, w3_ref, w2_ref, y_ref, acc_ref, *, SUB):
    w, f, s = pl.program_id(0), pl.program_id(1), pl.program_id(2)
    del ie_ref, ib_ref
    nsub = pl.num_programs(2)
    last_f = f == pl.num_programs(1) - 1
    n = ins_ref[w]

    @pl.when(last_f & (s >= nsub + n) & (n < 0))
    def _():
        y_ref[...] = jnp.zeros_like(y_ref)

    @pl.when((s >= nsub - n) & (n > 0))
    def _():
        x = x_ref[...]
        hid = _silu(_dot(x, w1_ref[0])) * _dot(x, w3_ref[0])
        part = _dot(hid, w2_ref[0])
        rows = pl.ds(pl.multiple_of((s - (nsub - n)) * SUB, SUB), SUB)

        @pl.when(f == 0)
        def _():
            acc_ref[rows, :] = part

        @pl.when(f > 0)
        def _():
            acc_ref[rows, :] += part

        @pl.when(last_f)
        def _():
            y_ref[...] = acc_ref[rows, :]


def _moe_experts(xs, item_e, item_b, item_n, w1, w3, w2, *, SUB, NSUB, tf=512):
    P, D = xs.shape
    E, _, DE = w1.shape
    NI = item_e.shape[0]
    nf = DE // tf

    def sub_blk(w, s, ib, ins):
        cnt = jnp.abs(ins[w])
        return ib[w] + jnp.clip(s - (NSUB - cnt), 0, jnp.maximum(cnt - 1, 0))

    return pl.pallas_call(
        functools.partial(_moe_expert_kernel, SUB=SUB),
        grid_spec=pltpu.PrefetchScalarGridSpec(
            num_scalar_prefetch=3, grid=(NI, nf, NSUB),
            in_specs=[
                pl.BlockSpec((SUB, D), lambda w, f, s, ie, ib, ins: (sub_blk(w, s, ib, ins), 0)),
                pl.BlockSpec((1, D, tf), lambda w, f, s, ie, ib, ins: (ie[w], 0, f)),
                pl.BlockSpec((1, D, tf), lambda w, f, s, ie, ib, ins: (ie[w], 0, f)),
                pl.BlockSpec((1, tf, D), lambda w, f, s, ie, ib, ins: (ie[w], f, 0)),
            ],
            out_specs=pl.BlockSpec(
                (SUB, D), lambda w, f, s, ie, ib, ins: (jnp.where(f == nf - 1, sub_blk(w, s, ib, ins), ib[w]), 0)),
            scratch_shapes=[pltpu.VMEM((NSUB * SUB, D), F32)]),
        out_shape=jax.ShapeDtypeStruct((P, D), F32),
        compiler_params=_cparams(("arbitrary",) * 3, 58),
        name="moe_experts",
    )(item_e, item_b, item_n, xs, w1, w3, w2)


def _moe_combine_kernel(pos_ref, h_ref, g_ref, nw_ref, y_hbm, o_ref, b0_ref, b1_ref, sem, *, tm, tok0):
    base = (tok0 + pl.program_id(0) * tm) * TOP_K

    def issue(grp, carry):
        for n in range(ISSUE_UNROLL):
            rr = grp * ISSUE_UNROLL + n
            for k, buf in enumerate((b0_ref, b1_ref)):
                pltpu.make_async_copy(y_hbm.at[pl.ds(pos_ref[base + TOP_K * rr + k], 1)], buf.at[pl.ds(rr, 1)],
                                      sem).start(priority=k)
        return carry

    lax.fori_loop(0, tm // ISSUE_UNROLL, issue, 0)
    pltpu.make_async_copy(y_hbm.at[pl.ds(0, tm)], b0_ref, sem).wait()
    pltpu.make_async_copy(y_hbm.at[pl.ds(0, tm)], b1_ref, sem).wait()
    g = g_ref[...]
    hf = h_ref[...] + (g[:, 0:1] * b0_ref[...] + g[:, 1:2] * b1_ref[...])
    o_ref[...] = hf * lax.rsqrt(jnp.mean(hf * hf, axis=-1, keepdims=True) + RMS_EPS) * nw_ref[...]


def _moe_combine(pos, h, gates, nw, y, tok0, *, tm=256):
    T, D = h.shape
    tm = min(tm, T)
    return pl.pallas_call(
        functools.partial(_moe_combine_kernel, tm=tm, tok0=tok0),
        grid_spec=pltpu.PrefetchScalarGridSpec(
            num_scalar_prefetch=1, grid=(T // tm,),
            in_specs=[pl.BlockSpec((tm, D), lambda i, p: (i, 0)),
                      pl.BlockSpec((tm, LANES), lambda i, p: (i, 0)),
                      pl.BlockSpec((1, D), lambda i, p: (0, 0)),
                      pl.BlockSpec(memory_space=pl.ANY)],
            out_specs=pl.BlockSpec((tm, D), lambda i, p: (i, 0)),
            scratch_shapes=[pltpu.VMEM((tm,) + y.shape[1:], F32), pltpu.VMEM((tm,) + y.shape[1:], F32),
                            pltpu.SemaphoreType.DMA(())]),
        out_shape=jax.ShapeDtypeStruct((T, D), F32),
        compiler_params=_cparams(("arbitrary",), 32),
        name="moe_combine",
    )(pos, h, gates, nw.reshape(1, D), y)


def _moe_plan(e_ids, *, SUB, NSUB):
    A = e_ids.shape[0]
    E = N_EXPERTS
    P = ((A + E * (SUB - 1) + SUB - 1) // SUB) * SUB
    order = jnp.argsort(e_ids, stable=True).astype(I32)
    counts = jnp.sum((e_ids[:, None] == jnp.arange(E, dtype=I32)[None, :]).astype(I32), axis=0)
    padded = ((counts + SUB - 1) // SUB) * SUB
    start = jnp.cumsum(counts) - counts
    pend = jnp.cumsum(padded)
    pstart = pend - padded
    se = e_ids[order]
    dest_sorted = pstart[se] + jnp.arange(A, dtype=I32) - start[se]
    pos = dest_sorted[jnp.argsort(order).astype(I32)]
    p = jnp.arange(P, dtype=I32)
    pe = jnp.minimum(jnp.sum((p[:, None] >= pend[None, :]).astype(I32), axis=1), E - 1)
    rank = p - pstart[pe]
    src = order[jnp.clip(start[pe] + rank, 0, A - 1)] // TOP_K
    slot_tok = jnp.where(rank < counts[pe], src, 0).astype(I32)
    nblk = padded // SUB
    n_items = (nblk + NSUB - 1) // NSUB
    iend = jnp.cumsum(n_items)
    n_blocks = P // SUB
    NI = E + (n_blocks + NSUB - 1) // NSUB
    w = jnp.arange(NI, dtype=I32)
    total = iend[-1]
    wc = jnp.minimum(w, total - 1)
    ie = jnp.minimum(jnp.sum((wc[:, None] >= iend[None, :]).astype(I32), axis=1), E - 1)
    k = wc - (iend[ie] - n_items[ie])
    ib = pstart[ie] // SUB + k * NSUB
    ins = jnp.minimum(NSUB, nblk[ie] - k * NSUB)
    valid = w < total
    tail0 = jnp.sum(nblk) + (w - total) * NSUB
    tail_n = jnp.clip(n_blocks - tail0, 0, NSUB)
    return (slot_tok, pos, ie.astype(I32), jnp.where(valid, ib, jnp.minimum(tail0, n_blocks - 1)).astype(I32),
            jnp.where(valid, ins, -tail_n).astype(I32))


def _mixer(x2, B, L, sb_fn, mem_k, mem_v, conv_prev, state0, lw, hn_all, row0, *, TL, CP, HB):
    (norm_mix_w, w_a, w_b, w_ab, dn_conv_w, dn_a_log, dn_dt_bias, dn_norm_w,
     w_sb_o, w_dn_o, w_x_o, w_out_bf, norm_ffn_w, w_router, b_router) = lw
    D = x2.shape[1]
    proj, ab = _proj(x2, norm_mix_w, w_a, COL_AB, w_b, w_ab)
    o_sb = sb_fn(proj)
    o_dn, dn_state = _deltanet(proj, ab, conv_prev, state0, dn_conv_w, dn_a_log, dn_dt_bias, dn_norm_w,
                               B, L, TL=TL, CP=CP, HB=HB)
    o_x = _cross_attn(proj, mem_k, mem_v, B, L)
    merged = _merge(o_sb, o_dn, o_x, w_sb_o, w_dn_o, w_x_o, proj, D)
    h, hn, ids, gates = _out_router(x2, merged, w_out_bf, norm_ffn_w, w_router, b_router, hn_all, row0)
    k_new = proj[:, COL_KSB:COL_KSB + SB_W].reshape(B, L, SB_HEADS, SB_HEAD_DIM)
    v_new = proj[:, COL_VSB:COL_VSB + SB_W].reshape(B, L, SB_HEADS, SB_HEAD_DIM)
    K = DN_CONV - 1
    tail = proj.reshape(B, L, proj.shape[1])[:, L - min(L, K):, COL_QDN:COL_ZDN]
    new_conv = jnp.concatenate([conv_prev, tail], axis=1)[:, -K:]
    return h, hn, ids, gates, k_new, v_new, dn_state, new_conv


def kernel(x_prompt, x_sample, cache_sb_k, cache_sb_v, cache_mem_k, cache_mem_v, state_dn, state_dn_conv,
           page_table, mem_prompt, norm_mix_w, w_in, sb_bias, dn_conv_w, dn_a_log, dn_dt_bias, dn_norm_w,
           mem_norm_w, w_mem_k, w_mem_v, w_sb_o, w_dn_o, w_x_o, w_out, norm_ffn_w, w_router_g, b_router_g,
           w_router_e, b_router_e, w_e1, w_e3, w_e2, norm_final_w):
    BP, S, D = x_prompt.shape
    DB, Q, _ = x_sample.shape
    depth = w_in.shape[0]
    n_mem = mem_prompt.shape[1]
    TP, TS = BP * S, DB * Q
    SUB, NSUB = 384, 4
    hp, hs = x_prompt.reshape(TP, D), x_sample.reshape(TS, D)
    outs = {k: [] for k in ("sb_kp", "sb_vp", "mem_kp", "mem_vp", "dn_sp", "dn_cp", "sb_ks", "sb_vs", "dn_ss", "dn_cs")}
    for l in range(depth):
        w_ab = jnp.pad(w_in[l][:, COL_AB:COL_AB + 2 * DN_HEADS], ((0, 0), (0, LANES - 2 * DN_HEADS)))
        w_a = w_in[l][:, :COL_AB].astype(BF16)
        w_b = w_in[l][:, COL_AB + 2 * DN_HEADS:].astype(BF16)
        n_r = N_GROUPS + N_EXPERTS
        w_router = jnp.pad(jnp.concatenate([w_router_g[l], w_router_e[l]], axis=1), ((0, 0), (0, LANES - n_r)))
        b_router = jnp.pad(jnp.concatenate([b_router_g[l], b_router_e[l]]), (0, LANES - n_r)).reshape(1, LANES)
        lw = (norm_mix_w[l], w_a, w_b, w_ab, dn_conv_w[l], dn_a_log[l], dn_dt_bias[l], dn_norm_w[l],
              w_sb_o[l], w_dn_o[l], w_x_o[l], w_out[l].astype(BF16), norm_ffn_w[l], w_router, b_router.astype(F32))
        mkv = _proj(mem_prompt.reshape(BP * n_mem, D), mem_norm_w[l], w_mem_k[l], X_W, w_mem_v[l], tm=512, tn=512)
        mk, mv = mkv[:, :X_W].reshape(BP, n_mem, X_W), mkv[:, X_W:].reshape(BP, n_mem, X_W)
        conv0 = jnp.zeros((BP, DN_CONV - 1, 3 * DN_W), F32)
        s0 = jnp.zeros((BP, DN_HEADS, DN_KDIM, DN_VDIM), F32)
        sbp = functools.partial(_sb_prompt, bias=sb_bias[l].astype(F32), B=BP, S=S)
        hp, hn_all, idp, gp, k_new, v_new, s_new, c_new = _mixer(hp, BP, S, sbp, mk, mv, conv0, s0, lw, TP + TS, 0,
                                                                 TL=256, CP=256, HB=4)
        outs["sb_kp"].append(k_new); outs["sb_vp"].append(v_new)
        outs["mem_kp"].append(mk.reshape(BP, n_mem, X_HEADS, X_HEAD_DIM))
        outs["mem_vp"].append(mv.reshape(BP, n_mem, X_HEADS, X_HEAD_DIM))
        outs["dn_sp"].append(s_new); outs["dn_cp"].append(c_new)
        sbs = functools.partial(_sb_sample, bias=sb_bias[l].astype(F32), cache_k=cache_sb_k, cache_v=cache_sb_v,
                                page_table=page_table, layer=l, DB=DB, Q=Q)
        hs, hn_all, ids_, gs, k_new, v_new, s_new, c_new = _mixer(
            hs, DB, Q, sbs, cache_mem_k[l].reshape(DB, n_mem, X_W), cache_mem_v[l].reshape(DB, n_mem, X_W),
            state_dn_conv[l], state_dn[l], lw, hn_all, TP, TL=Q, CP=LANES, HB=DN_HEADS)
        outs["sb_ks"].append(k_new); outs["sb_vs"].append(v_new)
        outs["dn_ss"].append(s_new); outs["dn_cs"].append(c_new)
        e_ids = jnp.concatenate([idp[:, :TOP_K], ids_[:, :TOP_K]], axis=0).reshape(-1)
        slot_tok, pos, ie, ib, ins = _moe_plan(e_ids, SUB=SUB, NSUB=NSUB)
        xs = _moe_gather(slot_tok, hn_all, GB=SUB)
        y = _moe_experts(xs, ie, ib, ins, w_e1[l], w_e3[l], w_e2[l], SUB=SUB, NSUB=NSUB)
        last = l == depth - 1
        nfw = norm_final_w if last else None
        assert last, "multi-layer stacking needs the un-normalised residual; only the final layer applies norm_final"
        hp = _moe_combine(pos, hp, gp, nfw, y, 0)
        hs = _moe_combine(pos, hs, gs, nfw, y, TP)
    stack = lambda k: jnp.stack(outs[k])
    return (hp.reshape(BP, S, D), hs.reshape(DB, Q, D), stack("sb_kp"), stack("sb_vp"), stack("mem_kp"),
            stack("mem_vp"), stack("dn_sp"), stack("dn_cp"), stack("sb_ks"), stack("sb_vs"), stack("dn_ss"),
            stack("dn_cs"))
```

```python
import functools

import jax
import jax.numpy as jnp
from jax import lax
from jax.experimental import pallas as pl
from jax.experimental.pallas import tpu as pltpu

F32, BF16, I32 = jnp.float32, jnp.bfloat16, jnp.int32

RMS_EPS = 1e-6
L2_EPS = 1e-6
SB_HEADS, SB_HEAD_DIM = 8, 128
DN_HEADS, DN_KDIM, DN_VDIM, DN_CONV = 8, 128, 128, 4
X_HEADS, X_HEAD_DIM = 4, 256
N_GROUPS, EXPERTS_PER_GROUP, TOP_K = 4, 8, 2
N_EXPERTS = N_GROUPS * EXPERTS_PER_GROUP
PAGE_SIZE = 128

LANES = 128
ISSUE_UNROLL = 8
MIB = 1024 * 1024
LOG2E = 1.4426950408889634

SB_W = SB_HEADS * SB_HEAD_DIM
DN_W = DN_HEADS * DN_KDIM
X_W = X_HEADS * X_HEAD_DIM
COL_QSB, COL_KSB, COL_VSB = 0, SB_W, 2 * SB_W
COL_QDN = 3 * SB_W
COL_KDN, COL_VDN, COL_ZDN = COL_QDN + DN_W, COL_QDN + 2 * DN_W, COL_QDN + 3 * DN_W
COL_AB = COL_ZDN + DN_W
COL_QX = COL_AB
NT = (((1,), (1,)), ((), ()))
TN = (((0,), (0,)), ((), ()))


def _cparams(sem, vmem_mib):
    return pltpu.CompilerParams(dimension_semantics=sem, vmem_limit_bytes=int(vmem_mib * MIB))


def _dot(a, b):
    return jnp.dot(a.astype(BF16), b.astype(BF16), preferred_element_type=F32)


def _dot_g(a, b, dims):
    return lax.dot_general(a.astype(BF16), b.astype(BF16), dims, preferred_element_type=F32)


def _split2(a):
    hi = a.astype(BF16)
    lo = (a - hi.astype(F32)).astype(BF16)
    return hi, lo


def _dot3(a, b):
    ah, al = _split2(a)
    bh, bl = _split2(b)
    d = functools.partial(jnp.dot, preferred_element_type=F32)
    return d(ah, bh) + (d(ah, bl) + d(al, bh))


def _softplus(z):
    return jnp.maximum(z, 0.0) + jnp.log(1.0 + jnp.exp(-jnp.abs(z)))


def _sigmoid(z):
    return 1.0 / (1.0 + jnp.exp(-z))


def _silu(z):
    return z * _sigmoid(z)


def _iota2(shape):
    return lax.broadcasted_iota(I32, shape, 0), lax.broadcasted_iota(I32, shape, 1)


def _div_pow2(x, p):
    assert p & (p - 1) == 0, p
    return lax.shift_right_logical(x, p.bit_length() - 1)


def _proj_kernel(*refs, na, has_extra):
    if has_extra:
        x_ref, nw_ref, wa_ref, wb_ref, we_ref, o_ref, e_ref, xn_ref = refs
    else:
        x_ref, nw_ref, wa_ref, wb_ref, o_ref, xn_ref = refs
    j = pl.program_id(1)

    @pl.when(j == 0)
    def _():
        x = x_ref[...]
        ms = jnp.mean(x * x, axis=-1, keepdims=True)
        xn = (x * lax.rsqrt(ms + RMS_EPS) * nw_ref[...]).astype(BF16)
        xn_ref[...] = xn
        if has_extra:
            e_ref[...] = jnp.dot(xn, we_ref[...].astype(BF16), preferred_element_type=F32)

    @pl.when(j < na)
    def _():
        o_ref[...] = jnp.dot(xn_ref[...], wa_ref[...].astype(BF16), preferred_element_type=F32)

    @pl.when(j >= na)
    def _():
        o_ref[...] = jnp.dot(xn_ref[...], wb_ref[...].astype(BF16), preferred_element_type=F32)


def _proj(x, nw, wa, na_cols, wb, w_extra=None, *, tm=1024, tn=1024):
    T, D = x.shape
    tm = min(tm, T)
    assert T % tm == 0 and na_cols % tn == 0 and wb.shape[1] % tn == 0
    na, nb = na_cols // tn, wb.shape[1] // tn
    has_extra = w_extra is not None
    in_specs = [
        pl.BlockSpec((tm, D), lambda i, j: (i, 0)),
        pl.BlockSpec((1, D), lambda i, j: (0, 0)),
        pl.BlockSpec((D, tn), lambda i, j: (0, jnp.minimum(j, na - 1))),
        pl.BlockSpec((D, tn), lambda i, j: (0, jnp.maximum(j - na, 0))),
    ]
    out_shape = [jax.ShapeDtypeStruct((T, (na + nb) * tn), F32)]
    out_specs = [pl.BlockSpec((tm, tn), lambda i, j: (i, j))]
    args = [x, nw.reshape(1, D), wa, wb]
    if has_extra:
        in_specs.append(pl.BlockSpec((D, LANES), lambda i, j: (0, 0)))
        out_shape.append(jax.ShapeDtypeStruct((T, LANES), F32))
        out_specs.append(pl.BlockSpec((tm, LANES), lambda i, j: (i, 0)))
        args.append(w_extra)
    wbytes = 2 * D * tn * (wa.dtype.itemsize + wb.dtype.itemsize)
    vmem = (2 * tm * D * 4 + tm * D * 2 + wbytes + 2 * tm * tn * 4 + 4 * MIB) / MIB + 6
    res = pl.pallas_call(
        functools.partial(_proj_kernel, na=na, has_extra=has_extra),
        grid=(T // tm, na + nb),
        in_specs=in_specs, out_specs=out_specs, out_shape=out_shape,
        scratch_shapes=[pltpu.VMEM((tm, D), BF16)],
        compiler_params=_cparams(("arbitrary", "arbitrary"), vmem),
        name="proj",
    )(*args)
    return res if has_extra else res[0]


def _sb_weights(z2s, masks, u2, acc):
    sp = [jnp.maximum(z, 0.0) + jnp.log(1.0 + jnp.exp2(jnp.minimum(z, -z))) * LOG2E for z in z2s]
    sp = [s if m is None else jnp.where(m, s, 0.0) for s, m in zip(sp, masks)]
    cum = [jnp.dot(s.astype(BF16), u2, preferred_element_type=F32) for s in sp]
    es = [z - c for z, c in zip(z2s, cum)]
    ws = []
    for e, c, m in zip(es, cum, masks):
        w = jnp.exp2(e - acc)
        ws.append(w if m is None else jnp.where(m, w, 0.0))
        acc = acc + c[:, 0:1]
    return ws, acc


def _later_key_matrix(n):
    r, c = _iota2((n, n))
    return jnp.where(r >= c, 1.0, 0.0).astype(BF16)


def _sb_prompt_kernel(bias_ref, q_ref, k_ref, v_ref, o_ref, kb_ref, vb_ref, *, tq, scale, head, unroll):
    h, i = pl.program_id(1), pl.program_id(2)

    @pl.when(i == 0)
    def _():
        kb_ref[...] = k_ref[...].astype(BF16)
        vb_ref[...] = v_ref[...].astype(BF16)

    bias2 = bias_ref[h] * LOG2E
    q = (q_ref[...] * (scale * LOG2E)).astype(BF16)
    u2 = _later_key_matrix(tq)
    r, c = _iota2((tq, tq))
    strict = c < r

    def run(blocks, masks, o, acc):
        sls = [pl.ds(pl.multiple_of(j * tq, tq), tq) for j in blocks]
        z2s = [lax.dot_general(q, kb_ref[sl, :], NT, preferred_element_type=F32) + bias2 for sl in sls]
        ws, acc = _sb_weights(z2s, masks, u2, acc)
        pvs = [jnp.dot(w.astype(BF16), vb_ref[sl, :], preferred_element_type=F32) for w, sl in zip(ws, sls)]
        while len(pvs) > 1:
            pvs = [a + b for a, b in zip(pvs[::2], pvs[1::2])] + ([pvs[-1]] if len(pvs) % 2 else [])
        return o + pvs[0], acc

    rem = jnp.bitwise_and(i, head - 1)
    zero = (jnp.zeros(o_ref.shape, F32), jnp.zeros((tq, 1), F32))
    variants = [functools.partial(lambda n: run([i - t for t in range(n + 1)], [strict] + [None] * n, *zero), n)
                for n in range(head)]
    carry = lax.switch(rem, variants)
    top = i - 1 - rem
    size = head
    while size < unroll:
        take = jnp.bitwise_and(_div_pow2(i, size), 1)
        group = functools.partial(lambda sz, tp, cr: run([tp - n for n in range(sz)], [None] * sz, *cr), size, top)
        carry = lax.cond(take == 1, group, lambda cr: cr, carry)
        top = top - take * size
        size *= 2

    def body(t, cr):
        j0 = top - t * unroll
        return run([j0 - n for n in range(unroll)], [None] * unroll, *cr)

    o, _ = lax.fori_loop(0, _div_pow2(i, unroll), body, carry)
    o_ref[...] = o.astype(o_ref.dtype)


def _sb_prompt(proj, bias, B, S, *, tq=256, head=4, unroll=8):
    H, d = SB_HEADS, SB_HEAD_DIM
    nq = S // tq
    return pl.pallas_call(
        functools.partial(_sb_prompt_kernel, tq=tq, scale=d ** -0.5, head=head, unroll=unroll),
        grid=(B, H, nq),
        in_specs=[
            pl.BlockSpec(memory_space=pltpu.SMEM),
            pl.BlockSpec((tq, d), lambda b, h, i: (b * nq + i, COL_QSB // d + h)),
            pl.BlockSpec((S, d), lambda b, h, i: (b, COL_KSB // d + h)),
            pl.BlockSpec((S, d), lambda b, h, i: (b, COL_VSB // d + h)),
        ],
        out_specs=pl.BlockSpec((tq, d), lambda b, h, i: (b * nq + i, h)),
        out_shape=jax.ShapeDtypeStruct((B * S, H * d), BF16),
        scratch_shapes=[pltpu.VMEM((S, d), BF16), pltpu.VMEM((S, d), BF16)],
        compiler_params=_cparams(("arbitrary",) * 3, 4 * S * d * 4 / MIB + 2 * S * d * 2 / MIB + 16),
        name="sb_prompt",
    )(bias, proj, proj, proj)


def _sb_sample_kernel(pt_ref, bias_ref, q_ref, kn_ref, vn_ref, *rest, G, scale):
    del pt_ref
    k_refs, v_refs = rest[:G], rest[G:2 * G]
    o_ref, acc_ref, out_ref = rest[2 * G:]
    H, d, P = SB_HEADS, SB_HEAD_DIM, PAGE_SIZE
    Q = q_ref.shape[0]
    HQ = H * Q
    g = pl.program_id(1)
    u2 = _later_key_matrix(P)
    qs = [(q_ref[:, h * d:(h + 1) * d] * (scale * LOG2E)).astype(BF16) for h in range(H)]

    def attend(tiles, acc, outs):
        z2s = [jnp.concatenate(
            [lax.dot_general(qs[h], kfn(h).astype(BF16), NT, preferred_element_type=F32) for h in range(H)],
            axis=0) + bias_ref[...] for kfn, _, _ in tiles]
        ws, acc = _sb_weights(z2s, [m for _, _, m in tiles], u2, acc)
        for w, (_, vfn, _) in zip(ws, tiles):
            outs = [outs[h] + jnp.dot(w[h * Q:(h + 1) * Q].astype(BF16), vfn(h).astype(BF16),
                                      preferred_element_type=F32) for h in range(H)]
        return acc, outs

    def finish(acc, outs):
        acc_ref[...] = acc
        out_ref[...] = jnp.concatenate(outs, axis=0)

    @pl.when(g == 0)
    def _():
        pad = jnp.zeros((P - Q, d), F32)
        rq, ck = _iota2((HQ, P))
        new = (lambda h: jnp.concatenate([kn_ref[:, h * d:(h + 1) * d], pad], axis=0),
               lambda h: jnp.concatenate([vn_ref[:, h * d:(h + 1) * d], pad], axis=0),
               ck < jnp.bitwise_and(rq, Q - 1))
        finish(*attend([new], jnp.zeros((HQ, 1), F32), [jnp.zeros((Q, d), F32)] * H))

    tiles = [(lambda h, t=t: k_refs[t][0, pl.ds(h, P, stride=H), :],
              lambda h, t=t: v_refs[t][0, pl.ds(h, P, stride=H), :], None) for t in range(G)]
    finish(*attend(tiles, acc_ref[...], [out_ref[h * Q:(h + 1) * Q, :] for h in range(H)]))

    @pl.when(g == pl.num_programs(1) - 1)
    def _():
        for h in range(H):
            o_ref[:, h * d:(h + 1) * d] = out_ref[h * Q:(h + 1) * Q, :].astype(o_ref.dtype)


def _sb_sample(proj, bias, cache_k, cache_v, page_table, layer, DB, Q, *, G=8):
    H, d = SB_HEADS, SB_HEAD_DIM
    W = H * d
    depth, n_pool = cache_k.shape[:2]
    NP = page_table.shape[1]
    ck = cache_k.reshape(depth * n_pool, PAGE_SIZE * H, d)
    cv = cache_v.reshape(depth * n_pool, PAGE_SIZE * H, d)
    base = layer * n_pool
    bias_rows = jnp.broadcast_to(jnp.repeat(bias * LOG2E, Q)[:, None], (H * Q, PAGE_SIZE)).astype(F32)

    def page_spec(t):
        return pl.BlockSpec((1, PAGE_SIZE * H, d), lambda b, g, pt: (base + pt[b, NP - 1 - (g * G + t)], 0, 0))

    in_specs = [
        pl.BlockSpec((H * Q, PAGE_SIZE), lambda b, g, pt: (0, 0)),
        pl.BlockSpec((Q, W), lambda b, g, pt: (b, COL_QSB // W)),
        pl.BlockSpec((Q, W), lambda b, g, pt: (b, COL_KSB // W)),
        pl.BlockSpec((Q, W), lambda b, g, pt: (b, COL_VSB // W)),
    ] + [page_spec(t) for t in range(G)] * 2
    return pl.pallas_call(
        functools.partial(_sb_sample_kernel, G=G, scale=d ** -0.5),
        grid_spec=pltpu.PrefetchScalarGridSpec(
            num_scalar_prefetch=1, grid=(DB, NP // G),
            in_specs=in_specs,
            out_specs=pl.BlockSpec((Q, W), lambda b, g, pt: (b, 0)),
            scratch_shapes=[pltpu.VMEM((H * Q, 1), F32), pltpu.VMEM((H * Q, d), F32)]),
        out_shape=jax.ShapeDtypeStruct((DB * Q, W), BF16),
        compiler_params=_cparams(("arbitrary", "arbitrary"), 4 * G * PAGE_SIZE * W * 4 / MIB + 12),
        name="sb_sample",
    )(page_table, bias_rows, proj, proj, proj, *([ck] * G), *([cv] * G))


def _unit_lower_inverse_m1(lms, n):
    r, c = _iota2((n, n))
    same = lambda s: _div_pow2(r, s) == _div_pow2(c, s)
    each = lambda f, *ls: [f(*a) for a in zip(*ls)]
    l16 = [jnp.where(same(16), lm, 0.0) for lm in lms]
    l2 = each(_dot, l16, l16)
    l4 = each(_dot, l2, l2)
    l8 = each(_dot, l4, l4)
    xr = [-a for a in l16]
    for p in (l2, l4, l8):
        xr = each(lambda x, q: x + q + _dot(x, q), xr, p)
    size = 32
    while size <= n:
        block = jnp.where(same(size), jnp.where(same(size // 2), 0.0, 1.0), 0.0)
        y = each(lambda x, lm: lm * block + _dot(x, lm * block), xr, lms)
        xr = each(lambda x, yy: x - yy - _dot(yy, x), xr, y)
        size *= 2
    return xr


def _dot2(a, b):
    ah, al = _split2(a)
    bh = b.astype(BF16)
    return jnp.dot(ah, bh, preferred_element_type=F32) + jnp.dot(al, bh, preferred_element_type=F32)


def _dn_kernel(q_ref, k_ref, v_ref, z_ref, ab_ref, cpq_ref, cpk_ref, cpv_ref, cwq_ref, cwk_ref, cwv_ref,
               s0_ref, alog_ref, dtb_ref, nw_ref, o_ref, s_out_ref, s_ref, prev_ref, seq_ref, *, TL, CP, HB):
    hg, i = pl.program_id(1), pl.program_id(2)
    K = DN_CONV - 1
    dk = DN_KDIM

    @pl.when(i == 0)
    def _():
        s_ref[...] = s0_ref[0]
        prev_ref[...] = jnp.zeros_like(prev_ref)
        for n, cp in enumerate((cpq_ref, cpk_ref, cpv_ref)):
            prev_ref[n, 8 - K:8, :] = cp[0]

    def conv(n, x_ref, cw_ref):
        seq_ref[0:8, :] = prev_ref[n]
        seq_ref[8:8 + TL, :] = x_ref[...]
        cw = cw_ref[...]
        acc = seq_ref[8:8 + TL, :] * cw[K:K + 1, :]
        for s in range(1, DN_CONV):
            acc = acc + seq_ref[8 - s:8 - s + TL, :] * cw[K - s:K - s + 1, :]
        prev_ref[n] = seq_ref[TL:TL + 8, :]
        y = _silu(acc)
        if CP > TL:
            y = jnp.concatenate([y, jnp.zeros((CP - TL, y.shape[1]), F32)], axis=0)
        return y

    qc_all = conv(0, q_ref, cwq_ref)
    kc_all = conv(1, k_ref, cwk_ref)
    vc_all = conv(2, v_ref, cwv_ref)
    ab = ab_ref[...]
    lane = lax.broadcasted_iota(I32, ab.shape, 1)
    r, c = _iota2((CP, CP))
    causal = r >= c
    tri = jnp.where(causal, 1.0, 0.0).astype(BF16)
    d = functools.partial(jnp.dot, preferred_element_type=F32)

    heads = range(HB)
    each = lambda f, *ls: [f(*a) for a in zip(*ls)]
    sls = [slice(hh * dk, (hh + 1) * dk) for hh in heads]
    l2n = lambda t: t * lax.rsqrt(jnp.sum(t * t, axis=-1, keepdims=True) + L2_EPS)
    qn = [l2n(qc_all[:, sl]) * (DN_KDIM ** -0.5) for sl in sls]
    kn = [l2n(kc_all[:, sl]) for sl in sls]
    vc = [vc_all[:, sl] for sl in sls]
    pick = lambda col: jnp.sum(jnp.where(lane == col, ab, 0.0), axis=-1, keepdims=True)
    pad0 = (lambda t: jnp.concatenate([t, jnp.zeros((CP - TL, 1), F32)], axis=0)) if CP > TL else (lambda t: t)
    beta = [pad0(_sigmoid(pick(DN_HEADS + hg * HB + hh))) for hh in heads]
    gl = [pad0(-jnp.exp(alog_ref[hg * HB + hh][:, 0:1]) * _softplus(pick(hg * HB + hh) + dtb_ref[hg * HB + hh][:, 0:1]))
          for hh in heads]

    def cumulative(g):
        g1 = jnp.broadcast_to(g, (CP, LANES))
        h1 = g1.astype(BF16)
        r1 = g1 - h1.astype(F32)
        h2 = r1.astype(BF16)
        h3 = (r1 - h2.astype(F32)).astype(BF16)
        return d(tri, h1) + (d(tri, h2) + d(tri, h3))

    gcum_b = [cumulative(g) for g in gl]
    gc = [t[:, 0:1] for t in gcum_b]
    gr = [jnp.transpose(t)[0:1, :] for t in gcum_b]
    decay = each(lambda a, b: jnp.where(causal, jnp.exp(jnp.where(causal, a - b, 0.0)), 0.0), gc, gr)
    kb = each(lambda k, b: k * b, kn, beta)
    lm = each(lambda a, b, dc: jnp.where(r > c, _dot_g(a, b, NT) * dc, 0.0), kb, kn, decay)
    qk = each(lambda a, b, dc: _dot_g(a, b, NT) * dc, qn, kn, decay)
    xr = _unit_lower_inverse_m1(lm, CP)
    eg = [jnp.exp(t) for t in gc]
    solve = lambda x, rhs: rhs + _dot2(x, rhs)
    u = each(lambda x, v, b: solve(x, v * b), xr, vc, beta)
    w = each(lambda x, k, e: solve(x, k * e), xr, kb, eg)
    s = [s_ref[hh] for hh in heads]
    v_new = each(lambda uu, ww, ss: uu - _dot(ww, ss), u, w, s)
    o = each(lambda q, e, ss, a, vn: _dot(q * e, ss) + _dot(a, vn), qn, eg, s, qk, v_new)
    g_last = [t[CP - 1:CP, :] for t in gc]
    s_new = each(lambda ss, gz, k, g, vn: ss * jnp.exp(gz) + _dot_g(k * jnp.exp(gz - g), vn, TN),
                 s, g_last, kn, gc, v_new)
    for hh in heads:
        s_ref[hh] = s_new[hh]
        oh = o[hh][:TL]
        oh = oh * lax.rsqrt(jnp.mean(oh * oh, axis=-1, keepdims=True) + RMS_EPS) * nw_ref[...]
        o_ref[:, sls[hh]] = (oh * _silu(z_ref[:, sls[hh]])).astype(o_ref.dtype)

    @pl.when(i == pl.num_programs(2) - 1)
    def _():
        s_out_ref[0] = s_ref[...]


def _deltanet(proj, ab, conv_prev, state0, conv_w, a_log, dt_bias, norm_w, B, L, *, TL, CP, HB):
    H, dk = DN_HEADS, DN_KDIM
    nl = L // TL
    wb = HB * dk
    cq, ck, cv, cz = COL_QDN // wb, COL_KDN // wb, COL_VDN // wb, COL_ZDN // wb
    row = lambda off: pl.BlockSpec((TL, wb), lambda b, h, i: (b * nl + i, off + h))
    cprev = lambda off: pl.BlockSpec((1, DN_CONV - 1, wb), lambda b, h, i: (b, 0, off + h))
    cwt = lambda off: pl.BlockSpec((DN_CONV, wb), lambda b, h, i: (0, off + h))
    per_head = pl.BlockSpec((H, 1, LANES), lambda b, h, i: (0, 0, 0))
    st = pl.BlockSpec((1, HB, dk, DN_VDIM), lambda b, h, i: (b, h, 0, 0))
    alog_b = jnp.broadcast_to(a_log.astype(F32)[:, None, None], (H, 1, LANES))
    dtb_b = jnp.broadcast_to(dt_bias.astype(F32)[:, None, None], (H, 1, LANES))
    o, s = pl.pallas_call(
        functools.partial(_dn_kernel, TL=TL, CP=CP, HB=HB),
        grid=(B, H // HB, nl),
        in_specs=[row(cq), row(ck), row(cv), row(cz),
                  pl.BlockSpec((TL, LANES), lambda b, h, i: (b * nl + i, 0)),
                  cprev(0), cprev(H // HB), cprev(2 * H // HB), cwt(0), cwt(H // HB), cwt(2 * H // HB),
                  st, per_head, per_head,
                  pl.BlockSpec((1, DN_VDIM), lambda b, h, i: (0, 0))],
        out_specs=[pl.BlockSpec((TL, wb), lambda b, h, i: (b * nl + i, h)), st],
        out_shape=[jax.ShapeDtypeStruct((B * L, H * DN_VDIM), BF16),
                   jax.ShapeDtypeStruct((B, H, dk, DN_VDIM), F32)],
        scratch_shapes=[pltpu.VMEM((HB, dk, DN_VDIM), F32), pltpu.VMEM((3, 8, wb), F32),
                        pltpu.VMEM((TL + 8, wb), F32)],
        compiler_params=_cparams(("arbitrary",) * 3, 48),
        name="deltanet",
    )(proj, proj, proj, proj, ab, conv_prev, conv_prev, conv_prev, conv_w, conv_w, conv_w,
      state0, alog_b, dtb_b, norm_w.reshape(1, DN_VDIM).astype(F32))
    return o, s


def _xattn_kernel(q_ref, mk_ref, mv_ref, o_ref):
    dx = X_HEAD_DIM
    for h in range(X_HEADS):
        sl = slice(h * dx, (h + 1) * dx)
        s = _dot_g(q_ref[:, sl], mk_ref[0][:, sl], NT) * (dx ** -0.5)
        e = jnp.exp(s - jnp.max(s, axis=-1, keepdims=True))
        p = e / jnp.sum(e, axis=-1, keepdims=True)
        o_ref[:, sl] = _dot(p, mv_ref[0][:, sl]).astype(o_ref.dtype)


def _cross_attn(proj, mem_k, mem_v, B, L, *, tm=512):
    tm = min(tm, L)
    nl = L // tm
    mem_blk = pl.BlockSpec((1,) + mem_k.shape[1:], lambda b, i: (b, 0, 0))
    return pl.pallas_call(
        _xattn_kernel,
        grid=(B, nl),
        in_specs=[pl.BlockSpec((tm, X_W), lambda b, i: (b * nl + i, COL_QX // X_W)), mem_blk, mem_blk],
        out_specs=pl.BlockSpec((tm, X_W), lambda b, i: (b * nl + i, 0)),
        out_shape=jax.ShapeDtypeStruct((B * L, X_W), BF16),
        compiler_params=_cparams(("arbitrary", "arbitrary"), 32),
        name="cross_attn",
    )(proj, mem_k, mem_v)


def _merge_kernel(osb_ref, odn_ref, ox_ref, wsb_ref, wdn_ref, wx_ref, gsb_ref, gdn_ref, gx_ref, o_ref):
    acc = _sigmoid(gsb_ref[...]) * _dot(osb_ref[...], wsb_ref[...])
    acc = acc + _sigmoid(gdn_ref[...]) * _dot(odn_ref[...], wdn_ref[...])
    acc = acc + _sigmoid(gx_ref[...]) * _dot(ox_ref[...], wx_ref[...])
    o_ref[...] = acc.astype(o_ref.dtype)


def _merge(o_sb, o_dn, o_x, w_sb_o, w_dn_o, w_x_o, proj, D, *, tm=1024, tn=512):
    T = o_sb.shape[0]
    tm = min(tm, T)
    assert T % tm == 0 and D % tn == 0
    g0 = (COL_QX + X_W) // tn
    act = lambda w: pl.BlockSpec((tm, w), lambda i, j: (i, 0))
    wt = lambda k: pl.BlockSpec((k, tn), lambda i, j: (0, j))
    gate = lambda n: pl.BlockSpec((tm, tn), lambda i, j: (i, g0 + n * (D // tn) + j))
    return pl.pallas_call(
        _merge_kernel,
        grid=(T // tm, D // tn),
        in_specs=[act(SB_W), act(DN_HEADS * DN_VDIM), act(X_W), wt(SB_W), wt(DN_HEADS * DN_VDIM), wt(X_W),
                  gate(0), gate(1), gate(2)],
        out_specs=pl.BlockSpec((tm, tn), lambda i, j: (i, j)),
        out_shape=jax.ShapeDtypeStruct((T, D), BF16),
        compiler_params=_cparams(("arbitrary", "arbitrary"), 48),
        name="merge",
    )(o_sb, o_dn, o_x, w_sb_o, w_dn_o, w_x_o, proj, proj, proj)


def _out_router_kernel(x_ref, m_ref, wo_ref, nw_ref, wr_ref, br_ref, *rest, n_main):
    h_ref, hn_ref, ids_ref, gates_ref = rest[-4:]

    @pl.when(pl.program_id(0) >= n_main)
    def _():
        hn_ref[...] = jnp.zeros_like(hn_ref)

    @pl.when(pl.program_id(0) < n_main)
    def _():
        _out_router_body(x_ref, m_ref, wo_ref, nw_ref, wr_ref, br_ref, h_ref, hn_ref, ids_ref, gates_ref)


def _out_router_body(x_ref, m_ref, wo_ref, nw_ref, wr_ref, br_ref, h_ref, hn_ref, ids_ref, gates_ref):
    hres = x_ref[...] + jnp.dot(m_ref[...], wo_ref[...], preferred_element_type=F32)
    h_ref[...] = hres
    hn = hres * lax.rsqrt(jnp.mean(hres * hres, axis=-1, keepdims=True) + RMS_EPS) * nw_ref[...]
    hn_ref[...] = hn
    logits = _dot3(hn, wr_ref[...]) + br_ref[...]
    lane_i = lax.broadcasted_iota(I32, logits.shape, 1)
    lane = lane_i.astype(F32)
    ninf = -jnp.inf
    first = lambda hit: jnp.min(jnp.where(hit, lane, float(LANES)), axis=-1, keepdims=True)
    gl = jnp.where(lane < N_GROUPS, logits, ninf)
    gmax = jnp.max(gl, axis=-1, keepdims=True)
    gidx = first(gl == gmax)
    g_p = 1.0 / jnp.sum(jnp.where(lane < N_GROUPS, jnp.exp(gl - gmax), 0.0), axis=-1, keepdims=True)
    lo = N_GROUPS + EXPERTS_PER_GROUP * gidx
    el = jnp.where(lane >= lo, jnp.where(lane < lo + EXPERTS_PER_GROUP, logits, ninf), ninf)
    m1 = jnp.max(el, axis=-1, keepdims=True)
    i1 = first(el == m1)
    el2 = jnp.where(lane == i1, ninf, el)
    m2 = jnp.max(el2, axis=-1, keepdims=True)
    i2 = first(el2 == m2)
    e2 = jnp.exp(m2 - m1)
    gate1 = g_p / (1.0 + e2)
    ids = jnp.where(lane_i == 0, i1 - N_GROUPS, jnp.where(lane_i == 1, i2 - N_GROUPS, 0.0))
    ids_ref[...] = ids.astype(I32)
    gates_ref[...] = jnp.where(lane_i == 0, gate1, jnp.where(lane_i == 1, gate1 * e2, 0.0))


def _out_router(x, merged, w_out_bf, nw, w_router, b_router, hn_all, row0, *, tm=512):
    T, D = x.shape
    tm = min(tm, T)
    assert T % tm == 0 and row0 % tm == 0
    n_main = T // tm
    rowf = pl.BlockSpec((tm, D), lambda i: (jnp.minimum(i, n_main - 1), 0))
    const = lambda s: pl.BlockSpec(s, lambda i: (0, 0))
    narrow = pl.BlockSpec((tm, LANES), lambda i: (jnp.minimum(i, n_main - 1), 0))
    in_specs = [rowf, rowf, const((D, D)), const((1, D)), const((D, LANES)), const((1, LANES))]
    args = [x, merged, w_out_bf, nw.reshape(1, D), w_router, b_router]
    if isinstance(hn_all, int):
        assert row0 == 0
        t_all, aliases = pl.cdiv(hn_all, tm) * tm, {}
    else:
        t_all, aliases = hn_all.shape[0], {len(args): 1}
        in_specs.append(pl.BlockSpec(memory_space=pl.ANY))
        args.append(hn_all)
        assert t_all % tm == 0 and row0 + T <= t_all
    n_steps = n_main if aliases else t_all // tm
    return pl.pallas_call(
        functools.partial(_out_router_kernel, n_main=n_main),
        grid=(n_steps,),
        in_specs=in_specs,
        out_specs=[rowf, pl.BlockSpec((tm, D), lambda i: (row0 // tm + i, 0)), narrow, narrow],
        out_shape=[jax.ShapeDtypeStruct((T, D), F32), jax.ShapeDtypeStruct((t_all, D), F32),
                   jax.ShapeDtypeStruct((T, LANES), I32), jax.ShapeDtypeStruct((T, LANES), F32)],
        input_output_aliases=aliases,
        compiler_params=_cparams(("arbitrary",), 56),
        name="out_router",
    )(*args)


def _moe_gather_kernel(tok_ref, hn_ref, o_ref, buf_ref, sem, *, GB):
    i, n = pl.program_id(0), pl.num_programs(0)
    slot = jnp.bitwise_and(i, 1)

    def start(blk, sl):
        def issue(grp, carry):
            for n in range(ISSUE_UNROLL):
                rr = grp * ISSUE_UNROLL + n
                tok = tok_ref[blk * GB + rr]
                pltpu.make_async_copy(hn_ref.at[pl.ds(tok, 1)], buf_ref.at[sl, pl.ds(rr, 1)],
                                      sem.at[sl]).start(priority=n % 2)
            return carry

        lax.fori_loop(0, GB // ISSUE_UNROLL, issue, 0)

    @pl.when(i == 0)
    def _():
        start(0, 0)

    @pl.when(i + 1 < n)
    def _():
        start(i + 1, 1 - slot)

    pltpu.make_async_copy(hn_ref.at[pl.ds(0, GB)], buf_ref.at[slot], sem.at[slot]).wait()
    o_ref[...] = buf_ref[slot].astype(o_ref.dtype)


def _moe_gather(slot_tok, hn, *, GB):
    P = slot_tok.shape[0]
    D = hn.shape[1]
    return pl.pallas_call(
        functools.partial(_moe_gather_kernel, GB=GB),
        grid_spec=pltpu.PrefetchScalarGridSpec(
            num_scalar_prefetch=1, grid=(P // GB,),
            in_specs=[pl.BlockSpec(memory_space=pl.ANY)],
            out_specs=pl.BlockSpec((GB, D), lambda i, tok: (i, 0)),
            scratch_shapes=[pltpu.VMEM((2, GB) + hn.shape[1:], F32), pltpu.SemaphoreType.DMA((2,))]),
        out_shape=jax.ShapeDtypeStruct((P, D), BF16),
        compiler_params=_cparams(("arbitrary",), 24),
        name="moe_gather",
    )(slot_tok, hn)


def _moe_expert_kernel(ie_ref, ib_ref, ins_ref, x_ref, w1_ref, w3_ref, w2_ref, y_ref, acc_ref, *, SUB):
    w, f, s = pl.program_id(0), pl.program_id(1), pl.program_id(2)
    del ie_ref, ib_ref
    nsub = pl.num_programs(2)
    last_f = f == pl.num_programs(1) - 1
    n = ins_ref[w]

    @pl.when(last_f & (s >= nsub + n) & (n < 0))
    def _():
        y_ref[...] = jnp.zeros_like(y_ref)

    @pl.when((s >= nsub - n) & (n > 0))
    def _():
        x = x_ref[...]
        hid = _silu(_dot(x, w1_ref[0])) * _dot(x, w3_ref[0])
        part = _dot(hid, w2_ref[0])
        rows = pl.ds(pl.multiple_of((s - (nsub - n)) * SUB, SUB), SUB)

        @pl.when(f == 0)
        def _():
            acc_ref[rows, :] = part

        @pl.when(f > 0)
        def _():
            acc_ref[rows, :] += part

        @pl.when(last_f)
        def _():
            y_ref[...] = acc_ref[rows, :]


def _moe_experts(xs, item_e, item_b, item_n, w1, w3, w2, *, SUB, NSUB, tf=512):
    P, D = xs.shape
    E, _, DE = w1.shape
    NI = item_e.shape[0]
    nf = DE // tf

    def sub_blk(w, s, ib, ins):
        cnt = jnp.abs(ins[w])
        return ib[w] + jnp.clip(s - (NSUB - cnt), 0, jnp.maximum(cnt - 1, 0))

    return pl.pallas_call(
        functools.partial(_moe_expert_kernel, SUB=SUB),
        grid_spec=pltpu.PrefetchScalarGridSpec(
            num_scalar_prefetch=3, grid=(NI, nf, NSUB),
            in_specs=[
                pl.BlockSpec((SUB, D), lambda w, f, s, ie, ib, ins: (sub_blk(w, s, ib, ins), 0)),
                pl.BlockSpec((1, D, tf), lambda w, f, s, ie, ib, ins: (ie[w], 0, f)),
                pl.BlockSpec((1, D, tf), lambda w, f, s, ie, ib, ins: (ie[w], 0, f)),
                pl.BlockSpec((1, tf, D), lambda w, f, s, ie, ib, ins: (ie[w], f, 0)),
            ],
            out_specs=pl.BlockSpec(
                (SUB, D), lambda w, f, s, ie, ib, ins: (jnp.where(f == nf - 1, sub_blk(w, s, ib, ins), ib[w]), 0)),
            scratch_shapes=[pltpu.VMEM((NSUB * SUB, D), F32)]),
        out_shape=jax.ShapeDtypeStruct((P, D), F32),
        compiler_params=_cparams(("arbitrary",) * 3, 58),
        name="moe_experts",
    )(item_e, item_b, item_n, xs, w1, w3, w2)


def _moe_combine_kernel(pos_ref, h_ref, g_ref, nw_ref, y_hbm, o_ref, b0_ref, b1_ref, sem, *, tm, tok0):
    base = (tok0 + pl.program_id(0) * tm) * TOP_K

    def issue(grp, carry):
        for n in range(ISSUE_UNROLL):
            rr = grp * ISSUE_UNROLL + n
            for k, buf in enumerate((b0_ref, b1_ref)):
                pltpu.make_async_copy(y_hbm.at[pl.ds(pos_ref[base + TOP_K * rr + k], 1)], buf.at[pl.ds(rr, 1)],
                                      sem).start(priority=k)
        return carry

    lax.fori_loop(0, tm // ISSUE_UNROLL, issue, 0)
    pltpu.make_async_copy(y_hbm.at[pl.ds(0, tm)], b0_ref, sem).wait()
    pltpu.make_async_copy(y_hbm.at[pl.ds(0, tm)], b1_ref, sem).wait()
    g = g_ref[...]
    hf = h_ref[...] + (g[:, 0:1] * b0_ref[...] + g[:, 1:2] * b1_ref[...])
    o_ref[...] = hf * lax.rsqrt(jnp.mean(hf * hf, axis=-1, keepdims=True) + RMS_EPS) * nw_ref[...]


def _moe_combine(pos, h, gates, nw, y, tok0, *, tm=256):
    T, D = h.shape
    tm = min(tm, T)
    return pl.pallas_call(
        functools.partial(_moe_combine_kernel, tm=tm, tok0=tok0),
        grid_spec=pltpu.PrefetchScalarGridSpec(
            num_scalar_prefetch=1, grid=(T // tm,),
            in_specs=[pl.BlockSpec((tm, D), lambda i, p: (i, 0)),
                      pl.BlockSpec((tm, LANES), lambda i, p: (i, 0)),
                      pl.BlockSpec((1, D), lambda i, p: (0, 0)),
                      pl.BlockSpec(memory_space=pl.ANY)],
            out_specs=pl.BlockSpec((tm, D), lambda i, p: (i, 0)),
            scratch_shapes=[pltpu.VMEM((tm,) + y.shape[1:], F32), pltpu.VMEM((tm,) + y.shape[1:], F32),
                            pltpu.SemaphoreType.DMA(())]),
        out_shape=jax.ShapeDtypeStruct((T, D), F32),
        compiler_params=_cparams(("arbitrary",), 32),
        name="moe_combine",
    )(pos, h, gates, nw.reshape(1, D), y)


def _moe_plan(e_ids, *, SUB, NSUB):
    A = e_ids.shape[0]
    E = N_EXPERTS
    P = ((A + E * (SUB - 1) + SUB - 1) // SUB) * SUB
    order = jnp.argsort(e_ids, stable=True).astype(I32)
    counts = jnp.sum((e_ids[:, None] == jnp.arange(E, dtype=I32)[None, :]).astype(I32), axis=0)
    padded = ((counts + SUB - 1) // SUB) * SUB
    start = jnp.cumsum(counts) - counts
    pend = jnp.cumsum(padded)
    pstart = pend - padded
    se = e_ids[order]
    dest_sorted = pstart[se] + jnp.arange(A, dtype=I32) - start[se]
    pos = dest_sorted[jnp.argsort(order).astype(I32)]
    p = jnp.arange(P, dtype=I32)
    pe = jnp.minimum(jnp.sum((p[:, None] >= pend[None, :]).astype(I32), axis=1), E - 1)
    rank = p - pstart[pe]
    src = order[jnp.clip(start[pe] + rank, 0, A - 1)] // TOP_K
    slot_tok = jnp.where(rank < counts[pe], src, 0).astype(I32)
    nblk = padded // SUB
    n_items = (nblk + NSUB - 1) // NSUB
    iend = jnp.cumsum(n_items)
    n_blocks = P // SUB
    NI = (n_blocks + E * (NSUB - 1) + NSUB - 1) // NSUB
    w = jnp.arange(NI, dtype=I32)
    total = iend[-1]
    wc = jnp.minimum(w, total - 1)
    ie = jnp.minimum(jnp.sum((wc[:, None] >= iend[None, :]).astype(I32), axis=1), E - 1)
    k = wc - (iend[ie] - n_items[ie])
    ib = pstart[ie] // SUB + k * NSUB
    ins = jnp.minimum(NSUB, nblk[ie] - k * NSUB)
    valid = w < total
    tail0 = jnp.sum(nblk) + (w - total) * NSUB
    tail_n = jnp.clip(n_blocks - tail0, 0, NSUB)
    return (slot_tok, pos, ie.astype(I32), jnp.where(valid, ib, jnp.minimum(tail0, n_blocks - 1)).astype(I32),
            jnp.where(valid, ins, -tail_n).astype(I32))


def _mixer(x2, B, L, sb_fn, mem_k, mem_v, conv_prev, state0, lw, hn_all, row0, *, TL, CP, HB):
    (norm_mix_w, w_a, w_b, w_ab, dn_conv_w, dn_a_log, dn_dt_bias, dn_norm_w,
     w_sb_o, w_dn_o, w_x_o, w_out_bf, norm_ffn_w, w_router, b_router) = lw
    D = x2.shape[1]
    proj, ab = _proj(x2, norm_mix_w, w_a, COL_AB, w_b, w_ab)
    o_sb = sb_fn(proj)
    o_dn, dn_state = _deltanet(proj, ab, conv_prev, state0, dn_conv_w, dn_a_log, dn_dt_bias, dn_norm_w,
                               B, L, TL=TL, CP=CP, HB=HB)
    o_x = _cross_attn(proj, mem_k, mem_v, B, L)
    merged = _merge(o_sb, o_dn, o_x, w_sb_o, w_dn_o, w_x_o, proj, D)
    h, hn, ids, gates = _out_router(x2, merged, w_out_bf, norm_ffn_w, w_router, b_router, hn_all, row0)
    k_new = proj[:, COL_KSB:COL_KSB + SB_W].reshape(B, L, SB_HEADS, SB_HEAD_DIM)
    v_new = proj[:, COL_VSB:COL_VSB + SB_W].reshape(B, L, SB_HEADS, SB_HEAD_DIM)
    K = DN_CONV - 1
    tail = proj.reshape(B, L, proj.shape[1])[:, L - min(L, K):, COL_QDN:COL_ZDN]
    new_conv = jnp.concatenate([conv_prev, tail], axis=1)[:, -K:]
    return h, hn, ids, gates, k_new, v_new, dn_state, new_conv


def kernel(x_prompt, x_sample, cache_sb_k, cache_sb_v, cache_mem_k, cache_mem_v, state_dn, state_dn_conv,
           page_table, mem_prompt, norm_mix_w, w_in, sb_bias, dn_conv_w, dn_a_log, dn_dt_bias, dn_norm_w,
           mem_norm_w, w_mem_k, w_mem_v, w_sb_o, w_dn_o, w_x_o, w_out, norm_ffn_w, w_router_g, b_router_g,
           w_router_e, b_router_e, w_e1, w_e3, w_e2, norm_final_w):
    BP, S, D = x_prompt.shape
    DB, Q, _ = x_sample.shape
    depth = w_in.shape[0]
    n_mem = mem_prompt.shape[1]
    TP, TS = BP * S, DB * Q
    SUB, NSUB = 384, 3
    hp, hs = x_prompt.reshape(TP, D), x_sample.reshape(TS, D)
    outs = {k: [] for k in ("sb_kp", "sb_vp", "mem_kp", "mem_vp", "dn_sp", "dn_cp", "sb_ks", "sb_vs", "dn_ss", "dn_cs")}
    for l in range(depth):
        w_ab = jnp.pad(w_in[l][:, COL_AB:COL_AB + 2 * DN_HEADS], ((0, 0), (0, LANES - 2 * DN_HEADS)))
        w_a = w_in[l][:, :COL_AB].astype(BF16)
        w_b = w_in[l][:, COL_AB + 2 * DN_HEADS:].astype(BF16)
        n_r = N_GROUPS + N_EXPERTS
        w_router = jnp.pad(jnp.concatenate([w_router_g[l], w_router_e[l]], axis=1), ((0, 0), (0, LANES - n_r)))
        b_router = jnp.pad(jnp.concatenate([b_router_g[l], b_router_e[l]]), (0, LANES - n_r)).reshape(1, LANES)
        lw = (norm_mix_w[l], w_a, w_b, w_ab, dn_conv_w[l], dn_a_log[l], dn_dt_bias[l], dn_norm_w[l],
              w_sb_o[l].astype(BF16), w_dn_o[l].astype(BF16), w_x_o[l].astype(BF16), w_out[l].astype(BF16),
              norm_ffn_w[l], w_router, b_router.astype(F32))
        mkv = _proj(mem_prompt.reshape(BP * n_mem, D), mem_norm_w[l], w_mem_k[l], X_W, w_mem_v[l], tm=512, tn=512)
        mk, mv = mkv[:, :X_W].reshape(BP, n_mem, X_W), mkv[:, X_W:].reshape(BP, n_mem, X_W)
        conv0 = jnp.zeros((BP, DN_CONV - 1, 3 * DN_W), F32)
        s0 = jnp.zeros((BP, DN_HEADS, DN_KDIM, DN_VDIM), F32)
        sbp = functools.partial(_sb_prompt, bias=sb_bias[l].astype(F32), B=BP, S=S)
        hp, hn_all, idp, gp, k_new, v_new, s_new, c_new = _mixer(hp, BP, S, sbp, mk, mv, conv0, s0, lw, TP + TS, 0,
                                                                 TL=256, CP=256, HB=4)
        outs["sb_kp"].append(k_new); outs["sb_vp"].append(v_new)
        outs["mem_kp"].append(mk.reshape(BP, n_mem, X_HEADS, X_HEAD_DIM))
        outs["mem_vp"].append(mv.reshape(BP, n_mem, X_HEADS, X_HEAD_DIM))
        outs["dn_sp"].append(s_new); outs["dn_cp"].append(c_new)
        sbs = functools.partial(_sb_sample, bias=sb_bias[l].astype(F32), cache_k=cache_sb_k, cache_v=cache_sb_v,
                                page_table=page_table, layer=l, DB=DB, Q=Q)
        hs, hn_all, ids_, gs, k_new, v_new, s_new, c_new = _mixer(
            hs, DB, Q, sbs, cache_mem_k[l].reshape(DB, n_mem, X_W), cache_mem_v[l].reshape(DB, n_mem, X_W),
            state_dn_conv[l], state_dn[l], lw, hn_all, TP, TL=Q, CP=LANES, HB=DN_HEADS)
        outs["sb_ks"].append(k_new); outs["sb_vs"].append(v_new)
        outs["dn_ss"].append(s_new); outs["dn_cs"].append(c_new)
        e_ids = jnp.concatenate([idp[:, :TOP_K], ids_[:, :TOP_K]], axis=0).reshape(-1)
        slot_tok, pos, ie, ib, ins = _moe_plan(e_ids, SUB=SUB, NSUB=NSUB)
        xs = _moe_gather(slot_tok, hn_all, GB=SUB)
        y = _moe_experts(xs, ie, ib, ins, w_e1[l], w_e3[l], w_e2[l], SUB=SUB, NSUB=NSUB)
        last = l == depth - 1
        nfw = norm_final_w if last else None
        assert last, "multi-layer stacking needs the un-normalised residual; only the final layer applies norm_final"
        hp = _moe_combine(pos, hp, gp, nfw, y, 0)
        hs = _moe_combine(pos, hs, gs, nfw, y, TP)
    stack = lambda k: jnp.stack(outs[k])
    return (hp.reshape(BP, S, D), hs.reshape(DB, Q, D), stack("sb_kp"), stack("sb_vp"), stack("mem_kp"),
            stack("mem_vp"), stack("dn_sp"), stack("dn_cp"), stack("sb_ks"), stack("sb_vs"), stack("dn_ss"),
            stack("dn_cs"))
```

```python
import functools

import jax
import jax.numpy as jnp
from jax import lax
from jax.experimental import pallas as pl
from jax.experimental.pallas import tpu as pltpu

F32, BF16, I32 = jnp.float32, jnp.bfloat16, jnp.int32

RMS_EPS = 1e-6
L2_EPS = 1e-6
SB_HEADS, SB_HEAD_DIM = 8, 128
DN_HEADS, DN_KDIM, DN_VDIM, DN_CONV = 8, 128, 128, 4
X_HEADS, X_HEAD_DIM = 4, 256
N_GROUPS, EXPERTS_PER_GROUP, TOP_K = 4, 8, 2
N_EXPERTS = N_GROUPS * EXPERTS_PER_GROUP
PAGE_SIZE = 128

LANES = 128
ISSUE_UNROLL = 8
MIB = 1024 * 1024
LOG2E = 1.4426950408889634

SB_W = SB_HEADS * SB_HEAD_DIM
DN_W = DN_HEADS * DN_KDIM
X_W = X_HEADS * X_HEAD_DIM
COL_QSB, COL_KSB, COL_VSB = 0, SB_W, 2 * SB_W
COL_QDN = 3 * SB_W
COL_KDN, COL_VDN, COL_ZDN = COL_QDN + DN_W, COL_QDN + 2 * DN_W, COL_QDN + 3 * DN_W
COL_AB = COL_ZDN + DN_W
COL_QX = COL_AB
NT = (((1,), (1,)), ((), ()))
TN = (((0,), (0,)), ((), ()))


def _cparams(sem, vmem_mib):
    return pltpu.CompilerParams(dimension_semantics=sem, vmem_limit_bytes=int(vmem_mib * MIB))


def _dot(a, b):
    return jnp.dot(a.astype(BF16), b.astype(BF16), preferred_element_type=F32)


def _dot_g(a, b, dims):
    return lax.dot_general(a.astype(BF16), b.astype(BF16), dims, preferred_element_type=F32)


def _split2(a):
    hi = a.astype(BF16)
    lo = (a - hi.astype(F32)).astype(BF16)
    return hi, lo


def _dot3(a, b):
    ah, al = _split2(a)
    bh, bl = _split2(b)
    d = functools.partial(jnp.dot, preferred_element_type=F32)
    return d(ah, bh) + (d(ah, bl) + d(al, bh))


def _softplus(z):
    return jnp.maximum(z, 0.0) + jnp.log(1.0 + jnp.exp(-jnp.abs(z)))


def _sigmoid(z):
    return 1.0 / (1.0 + jnp.exp(-z))


def _silu(z):
    return z * _sigmoid(z)


def _iota2(shape):
    return lax.broadcasted_iota(I32, shape, 0), lax.broadcasted_iota(I32, shape, 1)


def _div_pow2(x, p):
    assert p & (p - 1) == 0, p
    return lax.shift_right_logical(x, p.bit_length() - 1)


def _proj_kernel(*refs, na, has_extra):
    if has_extra:
        x_ref, nw_ref, wa_ref, wb_ref, we_ref, o_ref, e_ref, xn_ref = refs
    else:
        x_ref, nw_ref, wa_ref, wb_ref, o_ref, xn_ref = refs
    j = pl.program_id(1)

    @pl.when(j == 0)
    def _():
        x = x_ref[...]
        ms = jnp.mean(x * x, axis=-1, keepdims=True)
        xn = (x * lax.rsqrt(ms + RMS_EPS) * nw_ref[...]).astype(BF16)
        xn_ref[...] = xn
        if has_extra:
            e_ref[...] = jnp.dot(xn, we_ref[...].astype(BF16), preferred_element_type=F32)

    @pl.when(j < na)
    def _():
        o_ref[...] = jnp.dot(xn_ref[...], wa_ref[...].astype(BF16), preferred_element_type=F32)

    @pl.when(j >= na)
    def _():
        o_ref[...] = jnp.dot(xn_ref[...], wb_ref[...].astype(BF16), preferred_element_type=F32)


def _proj(x, nw, wa, na_cols, wb, w_extra=None, *, tm=1024, tn=1024):
    T, D = x.shape
    tm = min(tm, T)
    assert T % tm == 0 and na_cols % tn == 0 and wb.shape[1] % tn == 0
    na, nb = na_cols // tn, wb.shape[1] // tn
    has_extra = w_extra is not None
    in_specs = [
        pl.BlockSpec((tm, D), lambda i, j: (i, 0)),
        pl.BlockSpec((1, D), lambda i, j: (0, 0)),
        pl.BlockSpec((D, tn), lambda i, j: (0, jnp.minimum(j, na - 1))),
        pl.BlockSpec((D, tn), lambda i, j: (0, jnp.maximum(j - na, 0))),
    ]
    out_shape = [jax.ShapeDtypeStruct((T, (na + nb) * tn), F32)]
    out_specs = [pl.BlockSpec((tm, tn), lambda i, j: (i, j))]
    args = [x, nw.reshape(1, D), wa, wb]
    if has_extra:
        in_specs.append(pl.BlockSpec((D, LANES), lambda i, j: (0, 0)))
        out_shape.append(jax.ShapeDtypeStruct((T, LANES), F32))
        out_specs.append(pl.BlockSpec((tm, LANES), lambda i, j: (i, 0)))
        args.append(w_extra)
    wbytes = 2 * D * tn * (wa.dtype.itemsize + wb.dtype.itemsize)
    vmem = (2 * tm * D * 4 + tm * D * 2 + wbytes + 2 * tm * tn * 4 + 4 * MIB) / MIB + 6
    res = pl.pallas_call(
        functools.partial(_proj_kernel, na=na, has_extra=has_extra),
        grid=(T // tm, na + nb),
        in_specs=in_specs, out_specs=out_specs, out_shape=out_shape,
        scratch_shapes=[pltpu.VMEM((tm, D), BF16)],
        compiler_params=_cparams(("arbitrary", "arbitrary"), vmem),
        name="proj",
    )(*args)
    return res if has_extra else res[0]


def _sb_weights(z2s, masks, u2, acc):
    sp = [jnp.maximum(z, 0.0) + jnp.log(1.0 + jnp.exp2(jnp.minimum(z, -z))) * LOG2E for z in z2s]
    sp = [s if m is None else jnp.where(m, s, 0.0) for s, m in zip(sp, masks)]
    cum = [jnp.dot(s.astype(BF16), u2, preferred_element_type=F32) for s in sp]
    es = [z - c for z, c in zip(z2s, cum)]
    ws = []
    for e, c, m in zip(es, cum, masks):
        w = jnp.exp2(e - acc)
        ws.append(w if m is None else jnp.where(m, w, 0.0))
        acc = acc + c[:, 0:1]
    return ws, acc


def _later_key_matrix(n):
    r, c = _iota2((n, n))
    return jnp.where(r >= c, 1.0, 0.0).astype(BF16)


def _sb_prompt_kernel(bias_ref, q_ref, k_ref, v_ref, o_ref, kb_ref, vb_ref, *, tq, scale, head, unroll):
    h, i = pl.program_id(1), pl.program_id(2)

    @pl.when(i == 0)
    def _():
        kb_ref[...] = k_ref[...].astype(BF16)
        vb_ref[...] = v_ref[...].astype(BF16)

    bias2 = bias_ref[h] * LOG2E
    q = (q_ref[...] * (scale * LOG2E)).astype(BF16)
    u2 = _later_key_matrix(tq)
    r, c = _iota2((tq, tq))
    strict = c < r

    def run(blocks, masks, o, acc):
        sls = [pl.ds(pl.multiple_of(j * tq, tq), tq) for j in blocks]
        z2s = [lax.dot_general(q, kb_ref[sl, :], NT, preferred_element_type=F32) + bias2 for sl in sls]
        ws, acc = _sb_weights(z2s, masks, u2, acc)
        pvs = [jnp.dot(w.astype(BF16), vb_ref[sl, :], preferred_element_type=F32) for w, sl in zip(ws, sls)]
        while len(pvs) > 1:
            pvs = [a + b for a, b in zip(pvs[::2], pvs[1::2])] + ([pvs[-1]] if len(pvs) % 2 else [])
        return o + pvs[0], acc

    rem = jnp.bitwise_and(i, head - 1)
    zero = (jnp.zeros(o_ref.shape, F32), jnp.zeros((tq, 1), F32))
    variants = [functools.partial(lambda n: run([i - t for t in range(n + 1)], [strict] + [None] * n, *zero), n)
                for n in range(head)]
    carry = lax.switch(rem, variants)
    top = i - 1 - rem
    size = head
    while size < unroll:
        take = jnp.bitwise_and(_div_pow2(i, size), 1)
        group = functools.partial(lambda sz, tp, cr: run([tp - n for n in range(sz)], [None] * sz, *cr), size, top)
        carry = lax.cond(take == 1, group, lambda cr: cr, carry)
        top = top - take * size
        size *= 2

    def body(t, cr):
        j0 = top - t * unroll
        return run([j0 - n for n in range(unroll)], [None] * unroll, *cr)

    o, _ = lax.fori_loop(0, _div_pow2(i, unroll), body, carry)
    o_ref[...] = o.astype(o_ref.dtype)


def _sb_prompt(proj, bias, B, S, *, tq=256, head=8, unroll=8):
    H, d = SB_HEADS, SB_HEAD_DIM
    nq = S // tq
    return pl.pallas_call(
        functools.partial(_sb_prompt_kernel, tq=tq, scale=d ** -0.5, head=head, unroll=unroll),
        grid=(B, H, nq),
        in_specs=[
            pl.BlockSpec(memory_space=pltpu.SMEM),
            pl.BlockSpec((tq, d), lambda b, h, i: (b * nq + i, COL_QSB // d + h)),
            pl.BlockSpec((S, d), lambda b, h, i: (b, COL_KSB // d + h)),
            pl.BlockSpec((S, d), lambda b, h, i: (b, COL_VSB // d + h)),
        ],
        out_specs=pl.BlockSpec((tq, d), lambda b, h, i: (b * nq + i, h)),
        out_shape=jax.ShapeDtypeStruct((B * S, H * d), BF16),
        scratch_shapes=[pltpu.VMEM((S, d), BF16), pltpu.VMEM((S, d), BF16)],
        compiler_params=_cparams(("arbitrary",) * 3, 4 * S * d * 4 / MIB + 2 * S * d * 2 / MIB + 16),
        name="sb_prompt",
    )(bias, proj, proj, proj)


def _sb_sample_kernel(pt_ref, bias_ref, q_ref, kn_ref, vn_ref, *rest, G, scale):
    del pt_ref
    k_refs, v_refs = rest[:G], rest[G:2 * G]
    o_ref, acc_ref, out_ref = rest[2 * G:]
    H, d, P = SB_HEADS, SB_HEAD_DIM, PAGE_SIZE
    Q = q_ref.shape[0]
    HQ = H * Q
    g = pl.program_id(1)
    u2 = _later_key_matrix(P)
    qs = [(q_ref[:, h * d:(h + 1) * d] * (scale * LOG2E)).astype(BF16) for h in range(H)]

    def attend(tiles, acc, outs):
        z2s = [jnp.concatenate(
            [lax.dot_general(qs[h], kfn(h).astype(BF16), NT, preferred_element_type=F32) for h in range(H)],
            axis=0) + bias_ref[...] for kfn, _, _ in tiles]
        ws, acc = _sb_weights(z2s, [m for _, _, m in tiles], u2, acc)
        for w, (_, vfn, _) in zip(ws, tiles):
            outs = [outs[h] + jnp.dot(w[h * Q:(h + 1) * Q].astype(BF16), vfn(h).astype(BF16),
                                      preferred_element_type=F32) for h in range(H)]
        return acc, outs

    def finish(acc, outs):
        acc_ref[...] = acc
        out_ref[...] = jnp.concatenate(outs, axis=0)

    @pl.when(g == 0)
    def _():
        pad = jnp.zeros((P - Q, d), F32)
        rq, ck = _iota2((HQ, P))
        new = (lambda h: jnp.concatenate([kn_ref[:, h * d:(h + 1) * d], pad], axis=0),
               lambda h: jnp.concatenate([vn_ref[:, h * d:(h + 1) * d], pad], axis=0),
               ck < jnp.bitwise_and(rq, Q - 1))
        finish(*attend([new], jnp.zeros((HQ, 1), F32), [jnp.zeros((Q, d), F32)] * H))

    tiles = [(lambda h, t=t: k_refs[t][0, pl.ds(h, P, stride=H), :],
              lambda h, t=t: v_refs[t][0, pl.ds(h, P, stride=H), :], None) for t in range(G)]
    finish(*attend(tiles, acc_ref[...], [out_ref[h * Q:(h + 1) * Q, :] for h in range(H)]))

    @pl.when(g == pl.num_programs(1) - 1)
    def _():
        for h in range(H):
            o_ref[:, h * d:(h + 1) * d] = out_ref[h * Q:(h + 1) * Q, :].astype(o_ref.dtype)


def _sb_sample(proj, bias, cache_k, cache_v, page_table, layer, DB, Q, *, G=16):
    H, d = SB_HEADS, SB_HEAD_DIM
    W = H * d
    depth, n_pool = cache_k.shape[:2]
    NP = page_table.shape[1]
    assert NP % G == 0
    ck = cache_k.reshape(depth * n_pool, PAGE_SIZE * H, d)
    cv = cache_v.reshape(depth * n_pool, PAGE_SIZE * H, d)
    base = layer * n_pool
    bias_rows = jnp.broadcast_to(jnp.repeat(bias * LOG2E, Q)[:, None], (H * Q, PAGE_SIZE)).astype(F32)

    def page_spec(t):
        return pl.BlockSpec((1, PAGE_SIZE * H, d), lambda b, g, pt: (base + pt[b, NP - 1 - (g * G + t)], 0, 0))

    in_specs = [
        pl.BlockSpec((H * Q, PAGE_SIZE), lambda b, g, pt: (0, 0)),
        pl.BlockSpec((Q, W), lambda b, g, pt: (b, COL_QSB // W)),
        pl.BlockSpec((Q, W), lambda b, g, pt: (b, COL_KSB // W)),
        pl.BlockSpec((Q, W), lambda b, g, pt: (b, COL_VSB // W)),
    ] + [page_spec(t) for t in range(G)] * 2
    return pl.pallas_call(
        functools.partial(_sb_sample_kernel, G=G, scale=d ** -0.5),
        grid_spec=pltpu.PrefetchScalarGridSpec(
            num_scalar_prefetch=1, grid=(DB, NP // G),
            in_specs=in_specs,
            out_specs=pl.BlockSpec((Q, W), lambda b, g, pt: (b, 0)),
            scratch_shapes=[pltpu.VMEM((H * Q, 1), F32), pltpu.VMEM((H * Q, d), F32)]),
        out_shape=jax.ShapeDtypeStruct((DB * Q, W), BF16),
        compiler_params=_cparams(("arbitrary", "arbitrary"), 4 * G * PAGE_SIZE * W * 4 / MIB + 12),
        name="sb_sample",
    )(page_table, bias_rows, proj, proj, proj, *([ck] * G), *([cv] * G))


def _unit_lower_inverse_m1(lms, n):
    r, c = _iota2((n, n))
    same = lambda s: _div_pow2(r, s) == _div_pow2(c, s)
    each = lambda f, *ls: [f(*a) for a in zip(*ls)]
    l16 = [jnp.where(same(16), lm, 0.0) for lm in lms]
    l2 = each(_dot, l16, l16)
    l4 = each(_dot, l2, l2)
    l8 = each(_dot, l4, l4)
    xr = [-a for a in l16]
    for p in (l2, l4, l8):
        xr = each(lambda x, q: x + q + _dot(x, q), xr, p)
    size = 32
    while size <= n:
        block = jnp.where(same(size), jnp.where(same(size // 2), 0.0, 1.0), 0.0)
        y = each(lambda x, lm: lm * block + _dot(x, lm * block), xr, lms)
        xr = each(lambda x, yy: x - yy - _dot(yy, x), xr, y)
        size *= 2
    return xr


def _dot2(a, b):
    ah, al = _split2(a)
    bh = b.astype(BF16)
    return jnp.dot(ah, bh, preferred_element_type=F32) + jnp.dot(al, bh, preferred_element_type=F32)


def _dn_kernel(q_ref, k_ref, v_ref, z_ref, ab_ref, cpq_ref, cpk_ref, cpv_ref, cwq_ref, cwk_ref, cwv_ref,
               s0_ref, alog_ref, dtb_ref, nw_ref, o_ref, s_out_ref, s_ref, prev_ref, seq_ref, *, TL, CP, HB):
    hg, i = pl.program_id(1), pl.program_id(2)
    K = DN_CONV - 1
    dk = DN_KDIM

    @pl.when(i == 0)
    def _():
        s_ref[...] = s0_ref[0]
        prev_ref[...] = jnp.zeros_like(prev_ref)
        for n, cp in enumerate((cpq_ref, cpk_ref, cpv_ref)):
            prev_ref[n, 8 - K:8, :] = cp[0]

    def conv(n, x_ref, cw_ref):
        seq_ref[0:8, :] = prev_ref[n]
        seq_ref[8:8 + TL, :] = x_ref[...]
        cw = cw_ref[...]
        acc = seq_ref[8:8 + TL, :] * cw[K:K + 1, :]
        for s in range(1, DN_CONV):
            acc = acc + seq_ref[8 - s:8 - s + TL, :] * cw[K - s:K - s + 1, :]
        prev_ref[n] = seq_ref[TL:TL + 8, :]
        y = _silu(acc)
        if CP > TL:
            y = jnp.concatenate([y, jnp.zeros((CP - TL, y.shape[1]), F32)], axis=0)
        return y

    qc_all = conv(0, q_ref, cwq_ref)
    kc_all = conv(1, k_ref, cwk_ref)
    vc_all = conv(2, v_ref, cwv_ref)
    ab = ab_ref[...]
    lane = lax.broadcasted_iota(I32, ab.shape, 1)
    r, c = _iota2((CP, CP))
    causal = r >= c
    tri = jnp.where(causal, 1.0, 0.0).astype(BF16)
    d = functools.partial(jnp.dot, preferred_element_type=F32)

    heads = range(HB)
    each = lambda f, *ls: [f(*a) for a in zip(*ls)]
    sls = [slice(hh * dk, (hh + 1) * dk) for hh in heads]
    l2n = lambda t: t * lax.rsqrt(jnp.sum(t * t, axis=-1, keepdims=True) + L2_EPS)
    qn = [l2n(qc_all[:, sl]) * (DN_KDIM ** -0.5) for sl in sls]
    kn = [l2n(kc_all[:, sl]) for sl in sls]
    vc = [vc_all[:, sl] for sl in sls]
    pick = lambda col: jnp.sum(jnp.where(lane == col, ab, 0.0), axis=-1, keepdims=True)
    pad0 = (lambda t: jnp.concatenate([t, jnp.zeros((CP - TL, 1), F32)], axis=0)) if CP > TL else (lambda t: t)
    beta = [pad0(_sigmoid(pick(DN_HEADS + hg * HB + hh))) for hh in heads]
    gl = [pad0(-jnp.exp(alog_ref[hg * HB + hh][:, 0:1]) * _softplus(pick(hg * HB + hh) + dtb_ref[hg * HB + hh][:, 0:1]))
          for hh in heads]

    def cumulative(g):
        g1 = jnp.broadcast_to(g, (CP, LANES))
        h1 = g1.astype(BF16)
        r1 = g1 - h1.astype(F32)
        h2 = r1.astype(BF16)
        h3 = (r1 - h2.astype(F32)).astype(BF16)
        return d(tri, h1) + (d(tri, h2) + d(tri, h3))

    gcum_b = [cumulative(g) for g in gl]
    gc = [t[:, 0:1] for t in gcum_b]
    gr = [jnp.transpose(t)[0:1, :] for t in gcum_b]
    decay = each(lambda a, b: jnp.where(causal, jnp.exp(jnp.where(causal, a - b, 0.0)), 0.0), gc, gr)
    kb = each(lambda k, b: k * b, kn, beta)
    lm = each(lambda a, b, dc: jnp.where(r > c, _dot_g(a, b, NT) * dc, 0.0), kb, kn, decay)
    qk = each(lambda a, b, dc: _dot_g(a, b, NT) * dc, qn, kn, decay)
    xr = _unit_lower_inverse_m1(lm, CP)
    eg = [jnp.exp(t) for t in gc]
    solve = lambda x, rhs: rhs + _dot2(x, rhs)
    u = each(lambda x, v, b: solve(x, v * b), xr, vc, beta)
    w = each(lambda x, k, e: solve(x, k * e), xr, kb, eg)
    s = [s_ref[hh] for hh in heads]
    v_new = each(lambda uu, ww, ss: uu - _dot(ww, ss), u, w, s)
    o = each(lambda q, e, ss, a, vn: _dot(q * e, ss) + _dot(a, vn), qn, eg, s, qk, v_new)
    g_last = [t[CP - 1:CP, :] for t in gc]
    s_new = each(lambda ss, gz, k, g, vn: ss * jnp.exp(gz) + _dot_g(k * jnp.exp(gz - g), vn, TN),
                 s, g_last, kn, gc, v_new)
    for hh in heads:
        s_ref[hh] = s_new[hh]
        oh = o[hh][:TL]
        oh = oh * lax.rsqrt(jnp.mean(oh * oh, axis=-1, keepdims=True) + RMS_EPS) * nw_ref[...]
        o_ref[:, sls[hh]] = (oh * _silu(z_ref[:, sls[hh]])).astype(o_ref.dtype)

    @pl.when(i == pl.num_programs(2) - 1)
    def _():
        s_out_ref[0] = s_ref[...]


def _deltanet(proj, ab, conv_prev, state0, conv_w, a_log, dt_bias, norm_w, B, L, *, TL, CP, HB):
    H, dk = DN_HEADS, DN_KDIM
    nl = L // TL
    wb = HB * dk
    cq, ck, cv, cz = COL_QDN // wb, COL_KDN // wb, COL_VDN // wb, COL_ZDN // wb
    row = lambda off: pl.BlockSpec((TL, wb), lambda b, h, i: (b * nl + i, off + h))
    cprev = lambda off: pl.BlockSpec((1, DN_CONV - 1, wb), lambda b, h, i: (b, 0, off + h))
    cwt = lambda off: pl.BlockSpec((DN_CONV, wb), lambda b, h, i: (0, off + h))
    per_head = pl.BlockSpec((H, 1, LANES), lambda b, h, i: (0, 0, 0))
    st = pl.BlockSpec((1, HB, dk, DN_VDIM), lambda b, h, i: (b, h, 0, 0))
    alog_b = jnp.broadcast_to(a_log.astype(F32)[:, None, None], (H, 1, LANES))
    dtb_b = jnp.broadcast_to(dt_bias.astype(F32)[:, None, None], (H, 1, LANES))
    o, s = pl.pallas_call(
        functools.partial(_dn_kernel, TL=TL, CP=CP, HB=HB),
        grid=(B, H // HB, nl),
        in_specs=[row(cq), row(ck), row(cv), row(cz),
                  pl.BlockSpec((TL, LANES), lambda b, h, i: (b * nl + i, 0)),
                  cprev(0), cprev(H // HB), cprev(2 * H // HB), cwt(0), cwt(H // HB), cwt(2 * H // HB),
                  st, per_head, per_head,
                  pl.BlockSpec((1, DN_VDIM), lambda b, h, i: (0, 0))],
        out_specs=[pl.BlockSpec((TL, wb), lambda b, h, i: (b * nl + i, h)), st],
        out_shape=[jax.ShapeDtypeStruct((B * L, H * DN_VDIM), BF16),
                   jax.ShapeDtypeStruct((B, H, dk, DN_VDIM), F32)],
        scratch_shapes=[pltpu.VMEM((HB, dk, DN_VDIM), F32), pltpu.VMEM((3, 8, wb), F32),
                        pltpu.VMEM((TL + 8, wb), F32)],
        compiler_params=_cparams(("arbitrary",) * 3, 48),
        name="deltanet",
    )(proj, proj, proj, proj, ab, conv_prev, conv_prev, conv_prev, conv_w, conv_w, conv_w,
      state0, alog_b, dtb_b, norm_w.reshape(1, DN_VDIM).astype(F32))
    return o, s


def _xattn_kernel(q_ref, mk_ref, mv_ref, o_ref):
    dx = X_HEAD_DIM
    for h in range(X_HEADS):
        sl = slice(h * dx, (h + 1) * dx)
        s = _dot_g(q_ref[:, sl], mk_ref[0][:, sl], NT) * (dx ** -0.5)
        e = jnp.exp(s - jnp.max(s, axis=-1, keepdims=True))
        p = e / jnp.sum(e, axis=-1, keepdims=True)
        o_ref[:, sl] = _dot(p, mv_ref[0][:, sl]).astype(o_ref.dtype)


def _cross_attn(proj, mem_k, mem_v, B, L, *, tm=512):
    tm = min(tm, L)
    nl = L // tm
    mem_blk = pl.BlockSpec((1,) + mem_k.shape[1:], lambda b, i: (b, 0, 0))
    return pl.pallas_call(
        _xattn_kernel,
        grid=(B, nl),
        in_specs=[pl.BlockSpec((tm, X_W), lambda b, i: (b * nl + i, COL_QX // X_W)), mem_blk, mem_blk],
        out_specs=pl.BlockSpec((tm, X_W), lambda b, i: (b * nl + i, 0)),
        out_shape=jax.ShapeDtypeStruct((B * L, X_W), BF16),
        compiler_params=_cparams(("arbitrary", "arbitrary"), 32),
        name="cross_attn",
    )(proj, mem_k, mem_v)


def _merge_kernel(osb_ref, odn_ref, ox_ref, wsb_ref, wdn_ref, wx_ref, gsb_ref, gdn_ref, gx_ref, o_ref):
    acc = _sigmoid(gsb_ref[...]) * _dot(osb_ref[...], wsb_ref[...])
    acc = acc + _sigmoid(gdn_ref[...]) * _dot(odn_ref[...], wdn_ref[...])
    acc = acc + _sigmoid(gx_ref[...]) * _dot(ox_ref[...], wx_ref[...])
    o_ref[...] = acc.astype(o_ref.dtype)


def _merge(o_sb, o_dn, o_x, w_sb_o, w_dn_o, w_x_o, proj, D, *, tm=1024, tn=512):
    T = o_sb.shape[0]
    tm = min(tm, T)
    assert T % tm == 0 and D % tn == 0
    g0 = (COL_QX + X_W) // tn
    act = lambda w: pl.BlockSpec((tm, w), lambda i, j: (i, 0))
    wt = lambda k: pl.BlockSpec((k, tn), lambda i, j: (0, j))
    gate = lambda n: pl.BlockSpec((tm, tn), lambda i, j: (i, g0 + n * (D // tn) + j))
    return pl.pallas_call(
        _merge_kernel,
        grid=(T // tm, D // tn),
        in_specs=[act(SB_W), act(DN_HEADS * DN_VDIM), act(X_W), wt(SB_W), wt(DN_HEADS * DN_VDIM), wt(X_W),
                  gate(0), gate(1), gate(2)],
        out_specs=pl.BlockSpec((tm, tn), lambda i, j: (i, j)),
        out_shape=jax.ShapeDtypeStruct((T, D), BF16),
        compiler_params=_cparams(("arbitrary", "arbitrary"), 48),
        name="merge",
    )(o_sb, o_dn, o_x, w_sb_o, w_dn_o, w_x_o, proj, proj, proj)


def _out_router_kernel(x_ref, m_ref, wo_ref, nw_ref, wr_ref, br_ref, *rest, n_main):
    h_ref, hn_ref, ids_ref, gates_ref = rest[-4:]

    @pl.when(pl.program_id(0) >= n_main)
    def _():
        hn_ref[...] = jnp.zeros_like(hn_ref)

    @pl.when(pl.program_id(0) < n_main)
    def _():
        _out_router_body(x_ref, m_ref, wo_ref, nw_ref, wr_ref, br_ref, h_ref, hn_ref, ids_ref, gates_ref)


def _out_router_body(x_ref, m_ref, wo_ref, nw_ref, wr_ref, br_ref, h_ref, hn_ref, ids_ref, gates_ref):
    hres = x_ref[...] + jnp.dot(m_ref[...], wo_ref[...], preferred_element_type=F32)
    h_ref[...] = hres
    hn = hres * lax.rsqrt(jnp.mean(hres * hres, axis=-1, keepdims=True) + RMS_EPS) * nw_ref[...]
    hn_ref[...] = hn
    logits = _dot3(hn, wr_ref[...]) + br_ref[...]
    lane_i = lax.broadcasted_iota(I32, logits.shape, 1)
    lane = lane_i.astype(F32)
    ninf = -jnp.inf
    first = lambda hit: jnp.min(jnp.where(hit, lane, float(LANES)), axis=-1, keepdims=True)
    gl = jnp.where(lane < N_GROUPS, logits, ninf)
    gmax = jnp.max(gl, axis=-1, keepdims=True)
    gidx = first(gl == gmax)
    g_p = 1.0 / jnp.sum(jnp.where(lane < N_GROUPS, jnp.exp(gl - gmax), 0.0), axis=-1, keepdims=True)
    lo = N_GROUPS + EXPERTS_PER_GROUP * gidx
    el = jnp.where(lane >= lo, jnp.where(lane < lo + EXPERTS_PER_GROUP, logits, ninf), ninf)
    m1 = jnp.max(el, axis=-1, keepdims=True)
    i1 = first(el == m1)
    el2 = jnp.where(lane == i1, ninf, el)
    m2 = jnp.max(el2, axis=-1, keepdims=True)
    i2 = first(el2 == m2)
    e2 = jnp.exp(m2 - m1)
    gate1 = g_p / (1.0 + e2)
    ids = jnp.where(lane_i == 0, i1 - N_GROUPS, jnp.where(lane_i == 1, i2 - N_GROUPS, 0.0))
    ids_ref[...] = ids.astype(I32)
    gates_ref[...] = jnp.where(lane_i == 0, gate1, jnp.where(lane_i == 1, gate1 * e2, 0.0))


def _out_router(x, merged, w_out_bf, nw, w_router, b_router, hn_all, row0, *, tm=512):
    T, D = x.shape
    tm = min(tm, T)
    assert T % tm == 0 and row0 % tm == 0
    n_main = T // tm
    rowf = pl.BlockSpec((tm, D), lambda i: (jnp.minimum(i, n_main - 1), 0))
    const = lambda s: pl.BlockSpec(s, lambda i: (0, 0))
    narrow = pl.BlockSpec((tm, LANES), lambda i: (jnp.minimum(i, n_main - 1), 0))
    in_specs = [rowf, rowf, const((D, D)), const((1, D)), const((D, LANES)), const((1, LANES))]
    args = [x, merged, w_out_bf, nw.reshape(1, D), w_router, b_router]
    if isinstance(hn_all, int):
        assert row0 == 0
        t_all, aliases = pl.cdiv(hn_all, tm) * tm, {}
    else:
        t_all, aliases = hn_all.shape[0], {len(args): 1}
        in_specs.append(pl.BlockSpec(memory_space=pl.ANY))
        args.append(hn_all)
        assert t_all % tm == 0 and row0 + T <= t_all
    n_steps = n_main if aliases else t_all // tm
    return pl.pallas_call(
        functools.partial(_out_router_kernel, n_main=n_main),
        grid=(n_steps,),
        in_specs=in_specs,
        out_specs=[rowf, pl.BlockSpec((tm, D), lambda i: (row0 // tm + i, 0)), narrow, narrow],
        out_shape=[jax.ShapeDtypeStruct((T, D), F32), jax.ShapeDtypeStruct((t_all, D), F32),
                   jax.ShapeDtypeStruct((T, LANES), I32), jax.ShapeDtypeStruct((T, LANES), F32)],
        input_output_aliases=aliases,
        compiler_params=_cparams(("arbitrary",), 56),
        name="out_router",
    )(*args)


def _moe_gather_kernel(tok_ref, hn_ref, o_ref, buf_ref, sem, *, GB):
    i, n = pl.program_id(0), pl.num_programs(0)
    slot = jnp.bitwise_and(i, 1)

    def start(blk, sl):
        def issue(grp, carry):
            for n in range(ISSUE_UNROLL):
                rr = grp * ISSUE_UNROLL + n
                tok = tok_ref[blk * GB + rr]
                pltpu.make_async_copy(hn_ref.at[pl.ds(tok, 1)], buf_ref.at[sl, pl.ds(rr, 1)],
                                      sem.at[sl]).start(priority=n % 2)
            return carry

        lax.fori_loop(0, GB // ISSUE_UNROLL, issue, 0)

    @pl.when(i == 0)
    def _():
        start(0, 0)

    @pl.when(i + 1 < n)
    def _():
        start(i + 1, 1 - slot)

    pltpu.make_async_copy(hn_ref.at[pl.ds(0, GB)], buf_ref.at[slot], sem.at[slot]).wait()
    o_ref[...] = buf_ref[slot].astype(o_ref.dtype)


def _moe_gather(slot_tok, hn, *, GB):
    P = slot_tok.shape[0]
    D = hn.shape[1]
    return pl.pallas_call(
        functools.partial(_moe_gather_kernel, GB=GB),
        grid_spec=pltpu.PrefetchScalarGridSpec(
            num_scalar_prefetch=1, grid=(P // GB,),
            in_specs=[pl.BlockSpec(memory_space=pl.ANY)],
            out_specs=pl.BlockSpec((GB, D), lambda i, tok: (i, 0)),
            scratch_shapes=[pltpu.VMEM((2, GB) + hn.shape[1:], F32), pltpu.SemaphoreType.DMA((2,))]),
        out_shape=jax.ShapeDtypeStruct((P, D), BF16),
        compiler_params=_cparams(("arbitrary",), 24),
        name="moe_gather",
    )(slot_tok, hn)


def _moe_expert_kernel(ie_ref, ib_ref, ins_ref, x_ref, w1_ref, w3_ref, w2_ref, y_ref, acc_ref, *, SUB):
    w, f, s = pl.program_id(0), pl.program_id(1), pl.program_id(2)
    del ie_ref, ib_ref
    nsub = pl.num_programs(2)
    last_f = f == pl.num_programs(1) - 1
    n = ins_ref[w]

    @pl.when(last_f & (s >= nsub + n) & (n < 0))
    def _():
        y_ref[...] = jnp.zeros_like(y_ref)

    @pl.when((s >= nsub - n) & (n > 0))
    def _():
        x = x_ref[...]
        hid = _silu(_dot(x, w1_ref[0])) * _dot(x, w3_ref[0])
        part = _dot(hid, w2_ref[0])
        rows = pl.ds(pl.multiple_of((s - (nsub - n)) * SUB, SUB), SUB)

        @pl.when(f == 0)
        def _():
            acc_ref[rows, :] = part

        @pl.when(f > 0)
        def _():
            acc_ref[rows, :] += part

        @pl.when(last_f)
        def _():
            y_ref[...] = acc_ref[rows, :]


def _moe_experts(xs, item_e, item_b, item_n, w1, w3, w2, *, SUB, NSUB, tf=512):
    P, D = xs.shape
    E, _, DE = w1.shape
    NI = item_e.shape[0]
    nf = DE // tf

    def sub_blk(w, s, ib, ins):
        cnt = jnp.abs(ins[w])
        return ib[w] + jnp.clip(s - (NSUB - cnt), 0, jnp.maximum(cnt - 1, 0))

    return pl.pallas_call(
        functools.partial(_moe_expert_kernel, SUB=SUB),
        grid_spec=pltpu.PrefetchScalarGridSpec(
            num_scalar_prefetch=3, grid=(NI, nf, NSUB),
            in_specs=[
                pl.BlockSpec((SUB, D), lambda w, f, s, ie, ib, ins: (sub_blk(w, s, ib, ins), 0)),
                pl.BlockSpec((1, D, tf), lambda w, f, s, ie, ib, ins: (ie[w], 0, f)),
                pl.BlockSpec((1, D, tf), lambda w, f, s, ie, ib, ins: (ie[w], 0, f)),
                pl.BlockSpec((1, tf, D), lambda w, f, s, ie, ib, ins: (ie[w], f, 0)),
            ],
            out_specs=pl.BlockSpec(
                (SUB, D), lambda w, f, s, ie, ib, ins: (jnp.where(f == nf - 1, sub_blk(w, s, ib, ins), ib[w]), 0)),
            scratch_shapes=[pltpu.VMEM((NSUB * SUB, D), F32)]),
        out_shape=jax.ShapeDtypeStruct((P, D), F32),
        compiler_params=_cparams(("arbitrary",) * 3, 58),
        name="moe_experts",
    )(item_e, item_b, item_n, xs, w1, w3, w2)


def _moe_combine_kernel(pos_ref, h_ref, g_ref, nw_ref, y_hbm, o_ref, b0_ref, b1_ref, sem, *, tm, tok0):
    base = (tok0 + pl.program_id(0) * tm) * TOP_K

    def issue(grp, carry):
        for n in range(ISSUE_UNROLL):
            rr = grp * ISSUE_UNROLL + n
            for k, buf in enumerate((b0_ref, b1_ref)):
                pltpu.make_async_copy(y_hbm.at[pl.ds(pos_ref[base + TOP_K * rr + k], 1)], buf.at[pl.ds(rr, 1)],
                                      sem).start(priority=k)
        return carry

    lax.fori_loop(0, tm // ISSUE_UNROLL, issue, 0)
    pltpu.make_async_copy(y_hbm.at[pl.ds(0, tm)], b0_ref, sem).wait()
    pltpu.make_async_copy(y_hbm.at[pl.ds(0, tm)], b1_ref, sem).wait()
    g = g_ref[...]
    hf = h_ref[...] + (g[:, 0:1] * b0_ref[...] + g[:, 1:2] * b1_ref[...])
    o_ref[...] = hf * lax.rsqrt(jnp.mean(hf * hf, axis=-1, keepdims=True) + RMS_EPS) * nw_ref[...]


def _moe_combine(pos, h, gates, nw, y, tok0, *, tm=256):
    T, D = h.shape
    tm = min(tm, T)
    return pl.pallas_call(
        functools.partial(_moe_combine_kernel, tm=tm, tok0=tok0),
        grid_spec=pltpu.PrefetchScalarGridSpec(
            num_scalar_prefetch=1, grid=(T // tm,),
            in_specs=[pl.BlockSpec((tm, D), lambda i, p: (i, 0)),
                      pl.BlockSpec((tm, LANES), lambda i, p: (i, 0)),
                      pl.BlockSpec((1, D), lambda i, p: (0, 0)),
                      pl.BlockSpec(memory_space=pl.ANY)],
            out_specs=pl.BlockSpec((tm, D), lambda i, p: (i, 0)),
            scratch_shapes=[pltpu.VMEM((tm,) + y.shape[1:], F32), pltpu.VMEM((tm,) + y.shape[1:], F32),
                            pltpu.SemaphoreType.DMA(())]),
        out_shape=jax.ShapeDtypeStruct((T, D), F32),
        compiler_params=_cparams(("arbitrary",), 32),
        name="moe_combine",
    )(pos, h, gates, nw.reshape(1, D), y)


def _moe_plan(e_ids, *, SUB, NSUB):
    A = e_ids.shape[0]
    E = N_EXPERTS
    P = ((A + E * (SUB - 1) + SUB - 1) // SUB) * SUB
    order = jnp.argsort(e_ids, stable=True).astype(I32)
    counts = jnp.sum((e_ids[:, None] == jnp.arange(E, dtype=I32)[None, :]).astype(I32), axis=0)
    padded = ((counts + SUB - 1) // SUB) * SUB
    start = jnp.cumsum(counts) - counts
    pend = jnp.cumsum(padded)
    pstart = pend - padded
    se = e_ids[order]
    dest_sorted = pstart[se] + jnp.arange(A, dtype=I32) - start[se]
    pos = dest_sorted[jnp.argsort(order).astype(I32)]
    p = jnp.arange(P, dtype=I32)
    pe = jnp.minimum(jnp.sum((p[:, None] >= pend[None, :]).astype(I32), axis=1), E - 1)
    rank = p - pstart[pe]
    src = order[jnp.clip(start[pe] + rank, 0, A - 1)] // TOP_K
    slot_tok = jnp.where(rank < counts[pe], src, 0).astype(I32)
    nblk = padded // SUB
    n_items = (nblk + NSUB - 1) // NSUB
    iend = jnp.cumsum(n_items)
    n_blocks = P // SUB
    NI = (n_blocks + E * (NSUB - 1) + NSUB - 1) // NSUB
    w = jnp.arange(NI, dtype=I32)
    total = iend[-1]
    wc = jnp.minimum(w, total - 1)
    ie = jnp.minimum(jnp.sum((wc[:, None] >= iend[None, :]).astype(I32), axis=1), E - 1)
    k = wc - (iend[ie] - n_items[ie])
    ib = pstart[ie] // SUB + k * NSUB
    ins = jnp.minimum(NSUB, nblk[ie] - k * NSUB)
    valid = w < total
    tail0 = jnp.sum(nblk) + (w - total) * NSUB
    tail_n = jnp.clip(n_blocks - tail0, 0, NSUB)
    return (slot_tok, pos, ie.astype(I32), jnp.where(valid, ib, jnp.minimum(tail0, n_blocks - 1)).astype(I32),
            jnp.where(valid, ins, -tail_n).astype(I32))


def _mixer(x2, B, L, sb_fn, mem_k, mem_v, conv_prev, state0, lw, hn_all, row0, *, TL, CP, HB):
    (norm_mix_w, w_a, w_b, w_ab, dn_conv_w, dn_a_log, dn_dt_bias, dn_norm_w,
     w_sb_o, w_dn_o, w_x_o, w_out_bf, norm_ffn_w, w_router, b_router) = lw
    D = x2.shape[1]
    proj, ab = _proj(x2, norm_mix_w, w_a, COL_AB, w_b, w_ab)
    o_sb = sb_fn(proj)
    o_dn, dn_state = _deltanet(proj, ab, conv_prev, state0, dn_conv_w, dn_a_log, dn_dt_bias, dn_norm_w,
                               B, L, TL=TL, CP=CP, HB=HB)
    o_x = _cross_attn(proj, mem_k, mem_v, B, L)
    merged = _merge(o_sb, o_dn, o_x, w_sb_o, w_dn_o, w_x_o, proj, D)
    h, hn, ids, gates = _out_router(x2, merged, w_out_bf, norm_ffn_w, w_router, b_router, hn_all, row0)
    k_new = proj[:, COL_KSB:COL_KSB + SB_W].reshape(B, L, SB_HEADS, SB_HEAD_DIM)
    v_new = proj[:, COL_VSB:COL_VSB + SB_W].reshape(B, L, SB_HEADS, SB_HEAD_DIM)
    K = DN_CONV - 1
    tail = proj.reshape(B, L, proj.shape[1])[:, L - min(L, K):, COL_QDN:COL_ZDN]
    new_conv = jnp.concatenate([conv_prev, tail], axis=1)[:, -K:]
    return h, hn, ids, gates, k_new, v_new, dn_state, new_conv


def kernel(x_prompt, x_sample, cache_sb_k, cache_sb_v, cache_mem_k, cache_mem_v, state_dn, state_dn_conv,
           page_table, mem_prompt, norm_mix_w, w_in, sb_bias, dn_conv_w, dn_a_log, dn_dt_bias, dn_norm_w,
           mem_norm_w, w_mem_k, w_mem_v, w_sb_o, w_dn_o, w_x_o, w_out, norm_ffn_w, w_router_g, b_router_g,
           w_router_e, b_router_e, w_e1, w_e3, w_e2, norm_final_w):
    BP, S, D = x_prompt.shape
    DB, Q, _ = x_sample.shape
    depth = w_in.shape[0]
    n_mem = mem_prompt.shape[1]
    TP, TS = BP * S, DB * Q
    SUB, NSUB = 384, 3
    hp, hs = x_prompt.reshape(TP, D), x_sample.reshape(TS, D)
    outs = {k: [] for k in ("sb_kp", "sb_vp", "mem_kp", "mem_vp", "dn_sp", "dn_cp", "sb_ks", "sb_vs", "dn_ss", "dn_cs")}
    for l in range(depth):
        w_ab = jnp.pad(w_in[l][:, COL_AB:COL_AB + 2 * DN_HEADS], ((0, 0), (0, LANES - 2 * DN_HEADS)))
        w_a = w_in[l][:, :COL_AB].astype(BF16)
        w_b = w_in[l][:, COL_AB + 2 * DN_HEADS:].astype(BF16)
        n_r = N_GROUPS + N_EXPERTS
        w_router = jnp.pad(jnp.concatenate([w_router_g[l], w_router_e[l]], axis=1), ((0, 0), (0, LANES - n_r)))
        b_router = jnp.pad(jnp.concatenate([b_router_g[l], b_router_e[l]]), (0, LANES - n_r)).reshape(1, LANES)
        lw = (norm_mix_w[l], w_a, w_b, w_ab, dn_conv_w[l], dn_a_log[l], dn_dt_bias[l], dn_norm_w[l],
              w_sb_o[l].astype(BF16), w_dn_o[l].astype(BF16), w_x_o[l].astype(BF16), w_out[l].astype(BF16),
              norm_ffn_w[l], w_router, b_router.astype(F32))
        mkv = _proj(mem_prompt.reshape(BP * n_mem, D), mem_norm_w[l], w_mem_k[l], X_W, w_mem_v[l], tm=512, tn=512)
        mk, mv = mkv[:, :X_W].reshape(BP, n_mem, X_W), mkv[:, X_W:].reshape(BP, n_mem, X_W)
        conv0 = jnp.zeros((BP, DN_CONV - 1, 3 * DN_W), F32)
        s0 = jnp.zeros((BP, DN_HEADS, DN_KDIM, DN_VDIM), F32)
        sbp = functools.partial(_sb_prompt, bias=sb_bias[l].astype(F32), B=BP, S=S)
        hp, hn_all, idp, gp, k_new, v_new, s_new, c_new = _mixer(hp, BP, S, sbp, mk, mv, conv0, s0, lw, TP + TS, 0,
                                                                 TL=256, CP=256, HB=4)
        outs["sb_kp"].append(k_new); outs["sb_vp"].append(v_new)
        outs["mem_kp"].append(mk.reshape(BP, n_mem, X_HEADS, X_HEAD_DIM))
        outs["mem_vp"].append(mv.reshape(BP, n_mem, X_HEADS, X_HEAD_DIM))
        outs["dn_sp"].append(s_new); outs["dn_cp"].append(c_new)
        sbs = functools.partial(_sb_sample, bias=sb_bias[l].astype(F32), cache_k=cache_sb_k, cache_v=cache_sb_v,
                                page_table=page_table, layer=l, DB=DB, Q=Q)
        hs, hn_all, ids_, gs, k_new, v_new, s_new, c_new = _mixer(
            hs, DB, Q, sbs, cache_mem_k[l].reshape(DB, n_mem, X_W), cache_mem_v[l].reshape(DB, n_mem, X_W),
            state_dn_conv[l], state_dn[l], lw, hn_all, TP, TL=Q, CP=LANES, HB=DN_HEADS)
        outs["sb_ks"].append(k_new); outs["sb_vs"].append(v_new)
        outs["dn_ss"].append(s_new); outs["dn_cs"].append(c_new)
        e_ids = jnp.concatenate([idp[:, :TOP_K], ids_[:, :TOP_K]], axis=0).reshape(-1)
        slot_tok, pos, ie, ib, ins = _moe_plan(e_ids, SUB=SUB, NSUB=NSUB)
        xs = _moe_gather(slot_tok, hn_all, GB=SUB)
        y = _moe_experts(xs, ie, ib, ins, w_e1[l], w_e3[l], w_e2[l], SUB=SUB, NSUB=NSUB)
        last = l == depth - 1
        nfw = norm_final_w if last else None
        assert last, "multi-layer stacking needs the un-normalised residual; only the final layer applies norm_final"
        hp = _moe_combine(pos, hp, gp, nfw, y, 0)
        hs = _moe_combine(pos, hs, gs, nfw, y, TP)
    stack = lambda k: jnp.stack(outs[k])
    return (hp.reshape(BP, S, D), hs.reshape(DB, Q, D), stack("sb_kp"), stack("sb_vp"), stack("mem_kp"),
            stack("mem_vp"), stack("dn_sp"), stack("dn_cp"), stack("sb_ks"), stack("sb_vs"), stack("dn_ss"),
            stack("dn_cs"))
```

```python
import functools

import jax
import jax.numpy as jnp
from jax import lax
from jax.experimental import pallas as pl
from jax.experimental.pallas import tpu as pltpu

F32, BF16, I32 = jnp.float32, jnp.bfloat16, jnp.int32

RMS_EPS = 1e-6
L2_EPS = 1e-6
SB_HEADS, SB_HEAD_DIM = 8, 128
DN_HEADS, DN_KDIM, DN_VDIM, DN_CONV = 8, 128, 128, 4
X_HEADS, X_HEAD_DIM = 4, 256
N_GROUPS, EXPERTS_PER_GROUP, TOP_K = 4, 8, 2
N_EXPERTS = N_GROUPS * EXPERTS_PER_GROUP
PAGE_SIZE = 128

LANES = 128
ISSUE_UNROLL = 8
MIB = 1024 * 1024
LOG2E = 1.4426950408889634

SB_W = SB_HEADS * SB_HEAD_DIM
DN_W = DN_HEADS * DN_KDIM
X_W = X_HEADS * X_HEAD_DIM
COL_QSB, COL_KSB, COL_VSB = 0, SB_W, 2 * SB_W
COL_QDN = 3 * SB_W
COL_KDN, COL_VDN, COL_ZDN = COL_QDN + DN_W, COL_QDN + 2 * DN_W, COL_QDN + 3 * DN_W
COL_AB = COL_ZDN + DN_W
COL_QX = COL_AB
NT = (((1,), (1,)), ((), ()))
TN = (((0,), (0,)), ((), ()))


def _cparams(sem, vmem_mib):
    return pltpu.CompilerParams(dimension_semantics=sem, vmem_limit_bytes=int(vmem_mib * MIB))


def _dot(a, b):
    return jnp.dot(a.astype(BF16), b.astype(BF16), preferred_element_type=F32)


def _dot_g(a, b, dims):
    return lax.dot_general(a.astype(BF16), b.astype(BF16), dims, preferred_element_type=F32)


def _split2(a):
    hi = a.astype(BF16)
    lo = (a - hi.astype(F32)).astype(BF16)
    return hi, lo


def _dot3(a, b):
    ah, al = _split2(a)
    bh, bl = _split2(b)
    d = functools.partial(jnp.dot, preferred_element_type=F32)
    return d(ah, bh) + (d(ah, bl) + d(al, bh))


def _softplus(z):
    return jnp.maximum(z, 0.0) + jnp.log(1.0 + jnp.exp(-jnp.abs(z)))


def _sigmoid(z):
    return 1.0 / (1.0 + jnp.exp(-z))


def _silu(z):
    return z * _sigmoid(z)


def _iota2(shape):
    return lax.broadcasted_iota(I32, shape, 0), lax.broadcasted_iota(I32, shape, 1)


def _div_pow2(x, p):
    assert p & (p - 1) == 0, p
    return lax.shift_right_logical(x, p.bit_length() - 1)


def _proj_kernel(*refs, na, has_extra):
    if has_extra:
        x_ref, nw_ref, wa_ref, wb_ref, we_ref, o_ref, e_ref, xn_ref = refs
    else:
        x_ref, nw_ref, wa_ref, wb_ref, o_ref, xn_ref = refs
    j = pl.program_id(1)

    @pl.when(j == 0)
    def _():
        x = x_ref[...]
        ms = jnp.mean(x * x, axis=-1, keepdims=True)
        xn = (x * lax.rsqrt(ms + RMS_EPS) * nw_ref[...]).astype(BF16)
        xn_ref[...] = xn
        if has_extra:
            e_ref[...] = jnp.dot(xn, we_ref[...].astype(BF16), preferred_element_type=F32)

    @pl.when(j < na)
    def _():
        o_ref[...] = jnp.dot(xn_ref[...], wa_ref[...].astype(BF16), preferred_element_type=F32)

    @pl.when(j >= na)
    def _():
        o_ref[...] = jnp.dot(xn_ref[...], wb_ref[...].astype(BF16), preferred_element_type=F32)


def _proj(x, nw, wa, na_cols, wb, w_extra=None, *, tm=1024, tn=1024):
    T, D = x.shape
    tm = min(tm, T)
    assert T % tm == 0 and na_cols % tn == 0 and wb.shape[1] % tn == 0
    na, nb = na_cols // tn, wb.shape[1] // tn
    has_extra = w_extra is not None
    in_specs = [
        pl.BlockSpec((tm, D), lambda i, j: (i, 0)),
        pl.BlockSpec((1, D), lambda i, j: (0, 0)),
        pl.BlockSpec((D, tn), lambda i, j: (0, jnp.minimum(j, na - 1))),
        pl.BlockSpec((D, tn), lambda i, j: (0, jnp.maximum(j - na, 0))),
    ]
    out_shape = [jax.ShapeDtypeStruct((T, (na + nb) * tn), F32)]
    out_specs = [pl.BlockSpec((tm, tn), lambda i, j: (i, j))]
    args = [x, nw.reshape(1, D), wa, wb]
    if has_extra:
        in_specs.append(pl.BlockSpec((D, LANES), lambda i, j: (0, 0)))
        out_shape.append(jax.ShapeDtypeStruct((T, LANES), F32))
        out_specs.append(pl.BlockSpec((tm, LANES), lambda i, j: (i, 0)))
        args.append(w_extra)
    wbytes = 2 * D * tn * (wa.dtype.itemsize + wb.dtype.itemsize)
    vmem = (2 * tm * D * 4 + tm * D * 2 + wbytes + 2 * tm * tn * 4 + 4 * MIB) / MIB + 6
    res = pl.pallas_call(
        functools.partial(_proj_kernel, na=na, has_extra=has_extra),
        grid=(T // tm, na + nb),
        in_specs=in_specs, out_specs=out_specs, out_shape=out_shape,
        scratch_shapes=[pltpu.VMEM((tm, D), BF16)],
        compiler_params=_cparams(("arbitrary", "arbitrary"), vmem),
        name="proj",
    )(*args)
    return res if has_extra else res[0]


def _sb_weights(z2s, masks, u2, acc):
    sp = [jnp.maximum(z, 0.0) + jnp.log(1.0 + jnp.exp2(jnp.minimum(z, -z))) * LOG2E for z in z2s]
    sp = [s if m is None else jnp.where(m, s, 0.0) for s, m in zip(sp, masks)]
    cum = [jnp.dot(s.astype(BF16), u2, preferred_element_type=F32) for s in sp]
    es = [z - c for z, c in zip(z2s, cum)]
    ws = []
    for e, c, m in zip(es, cum, masks):
        w = jnp.exp2(e - acc)
        ws.append(w if m is None else jnp.where(m, w, 0.0))
        acc = acc + c[:, 0:1]
    return ws, acc


def _later_key_matrix(n):
    r, c = _iota2((n, n))
    return jnp.where(r >= c, 1.0, 0.0).astype(BF16)


def _sb_prompt_kernel(bias_ref, q_ref, k_ref, v_ref, o_ref, kb_ref, vb_ref, *, tq, scale, head, unroll):
    h, i = pl.program_id(1), pl.program_id(2)

    @pl.when(i == 0)
    def _():
        kb_ref[...] = k_ref[...].astype(BF16)
        vb_ref[...] = v_ref[...].astype(BF16)

    bias2 = bias_ref[h] * LOG2E
    q = (q_ref[...] * (scale * LOG2E)).astype(BF16)
    u2 = _later_key_matrix(tq)
    r, c = _iota2((tq, tq))
    strict = c < r

    def run(blocks, masks, o, acc):
        sls = [pl.ds(pl.multiple_of(j * tq, tq), tq) for j in blocks]
        z2s = [lax.dot_general(q, kb_ref[sl, :], NT, preferred_element_type=F32) + bias2 for sl in sls]
        ws, acc = _sb_weights(z2s, masks, u2, acc)
        pvs = [jnp.dot(w.astype(BF16), vb_ref[sl, :], preferred_element_type=F32) for w, sl in zip(ws, sls)]
        while len(pvs) > 1:
            pvs = [a + b for a, b in zip(pvs[::2], pvs[1::2])] + ([pvs[-1]] if len(pvs) % 2 else [])
        return o + pvs[0], acc

    rem = jnp.bitwise_and(i, head - 1)
    zero = (jnp.zeros(o_ref.shape, F32), jnp.zeros((tq, 1), F32))
    variants = [functools.partial(lambda n: run([i - t for t in range(n + 1)], [strict] + [None] * n, *zero), n)
                for n in range(head)]
    carry = lax.switch(rem, variants)
    top = i - 1 - rem
    size = head
    while size < unroll:
        take = jnp.bitwise_and(_div_pow2(i, size), 1)
        group = functools.partial(lambda sz, tp, cr: run([tp - n for n in range(sz)], [None] * sz, *cr), size, top)
        carry = lax.cond(take == 1, group, lambda cr: cr, carry)
        top = top - take * size
        size *= 2

    def body(t, cr):
        j0 = top - t * unroll
        return run([j0 - n for n in range(unroll)], [None] * unroll, *cr)

    o, _ = lax.fori_loop(0, _div_pow2(i, unroll), body, carry)
    o_ref[...] = o.astype(o_ref.dtype)


def _sb_prompt(proj, bias, B, S, *, tq=256, head=8, unroll=16):
    H, d = SB_HEADS, SB_HEAD_DIM
    nq = S // tq
    return pl.pallas_call(
        functools.partial(_sb_prompt_kernel, tq=tq, scale=d ** -0.5, head=head, unroll=unroll),
        grid=(B, H, nq),
        in_specs=[
            pl.BlockSpec(memory_space=pltpu.SMEM),
            pl.BlockSpec((tq, d), lambda b, h, i: (b * nq + i, COL_QSB // d + h)),
            pl.BlockSpec((S, d), lambda b, h, i: (b, COL_KSB // d + h)),
            pl.BlockSpec((S, d), lambda b, h, i: (b, COL_VSB // d + h)),
        ],
        out_specs=pl.BlockSpec((tq, d), lambda b, h, i: (b * nq + i, h)),
        out_shape=jax.ShapeDtypeStruct((B * S, H * d), BF16),
        scratch_shapes=[pltpu.VMEM((S, d), BF16), pltpu.VMEM((S, d), BF16)],
        compiler_params=_cparams(("arbitrary",) * 3, 4 * S * d * 4 / MIB + 2 * S * d * 2 / MIB + 16),
        name="sb_prompt",
    )(bias, proj, proj, proj)


def _sb_sample_kernel(pt_ref, bias_ref, q_ref, kn_ref, vn_ref, *rest, G, scale):
    del pt_ref
    k_refs, v_refs = rest[:G], rest[G:2 * G]
    o_ref, acc_ref, out_ref = rest[2 * G:]
    H, d, P = SB_HEADS, SB_HEAD_DIM, PAGE_SIZE
    Q = q_ref.shape[0]
    HQ = H * Q
    g = pl.program_id(1)
    u2 = _later_key_matrix(P)
    qs = [(q_ref[:, h * d:(h + 1) * d] * (scale * LOG2E)).astype(BF16) for h in range(H)]

    def attend(tiles, acc, outs):
        z2s = [jnp.concatenate(
            [lax.dot_general(qs[h], kfn(h).astype(BF16), NT, preferred_element_type=F32) for h in range(H)],
            axis=0) + bias_ref[...] for kfn, _, _ in tiles]
        ws, acc = _sb_weights(z2s, [m for _, _, m in tiles], u2, acc)
        for w, (_, vfn, _) in zip(ws, tiles):
            outs = [outs[h] + jnp.dot(w[h * Q:(h + 1) * Q].astype(BF16), vfn(h).astype(BF16),
                                      preferred_element_type=F32) for h in range(H)]
        return acc, outs

    def finish(acc, outs):
        acc_ref[...] = acc
        out_ref[...] = jnp.concatenate(outs, axis=0)

    @pl.when(g == 0)
    def _():
        pad = jnp.zeros((P - Q, d), F32)
        rq, ck = _iota2((HQ, P))
        new = (lambda h: jnp.concatenate([kn_ref[:, h * d:(h + 1) * d], pad], axis=0),
               lambda h: jnp.concatenate([vn_ref[:, h * d:(h + 1) * d], pad], axis=0),
               ck < jnp.bitwise_and(rq, Q - 1))
        finish(*attend([new], jnp.zeros((HQ, 1), F32), [jnp.zeros((Q, d), F32)] * H))

    tiles = [(lambda h, t=t: k_refs[t][0, pl.ds(h, P, stride=H), :],
              lambda h, t=t: v_refs[t][0, pl.ds(h, P, stride=H), :], None) for t in range(G)]
    finish(*attend(tiles, acc_ref[...], [out_ref[h * Q:(h + 1) * Q, :] for h in range(H)]))

    @pl.when(g == pl.num_programs(1) - 1)
    def _():
        for h in range(H):
            o_ref[:, h * d:(h + 1) * d] = out_ref[h * Q:(h + 1) * Q, :].astype(o_ref.dtype)


def _sb_sample(proj, bias, cache_k, cache_v, page_table, layer, DB, Q, *, G=16):
    H, d = SB_HEADS, SB_HEAD_DIM
    W = H * d
    depth, n_pool = cache_k.shape[:2]
    NP = page_table.shape[1]
    assert NP % G == 0
    ck = cache_k.reshape(depth * n_pool, PAGE_SIZE * H, d)
    cv = cache_v.reshape(depth * n_pool, PAGE_SIZE * H, d)
    base = layer * n_pool
    bias_rows = jnp.broadcast_to(jnp.repeat(bias * LOG2E, Q)[:, None], (H * Q, PAGE_SIZE)).astype(F32)

    def page_spec(t):
        return pl.BlockSpec((1, PAGE_SIZE * H, d), lambda b, g, pt: (base + pt[b, NP - 1 - (g * G + t)], 0, 0))

    in_specs = [
        pl.BlockSpec((H * Q, PAGE_SIZE), lambda b, g, pt: (0, 0)),
        pl.BlockSpec((Q, W), lambda b, g, pt: (b, COL_QSB // W)),
        pl.BlockSpec((Q, W), lambda b, g, pt: (b, COL_KSB // W)),
        pl.BlockSpec((Q, W), lambda b, g, pt: (b, COL_VSB // W)),
    ] + [page_spec(t) for t in range(G)] * 2
    return pl.pallas_call(
        functools.partial(_sb_sample_kernel, G=G, scale=d ** -0.5),
        grid_spec=pltpu.PrefetchScalarGridSpec(
            num_scalar_prefetch=1, grid=(DB, NP // G),
            in_specs=in_specs,
            out_specs=pl.BlockSpec((Q, W), lambda b, g, pt: (b, 0)),
            scratch_shapes=[pltpu.VMEM((H * Q, 1), F32), pltpu.VMEM((H * Q, d), F32)]),
        out_shape=jax.ShapeDtypeStruct((DB * Q, W), BF16),
        compiler_params=_cparams(("arbitrary", "arbitrary"), 4 * G * PAGE_SIZE * W * 4 / MIB + 12),
        name="sb_sample",
    )(page_table, bias_rows, proj, proj, proj, *([ck] * G), *([cv] * G))


def _unit_lower_inverse_m1(lms, n):
    r, c = _iota2((n, n))
    same = lambda s: _div_pow2(r, s) == _div_pow2(c, s)
    each = lambda f, *ls: [f(*a) for a in zip(*ls)]
    l16 = [jnp.where(same(16), lm, 0.0) for lm in lms]
    l2 = each(_dot, l16, l16)
    l4 = each(_dot, l2, l2)
    l8 = each(_dot, l4, l4)
    xr = [-a for a in l16]
    for p in (l2, l4, l8):
        xr = each(lambda x, q: x + q + _dot(x, q), xr, p)
    size = 32
    while size <= n:
        block = jnp.where(same(size), jnp.where(same(size // 2), 0.0, 1.0), 0.0)
        y = each(lambda x, lm: lm * block + _dot(x, lm * block), xr, lms)
        xr = each(lambda x, yy: x - yy - _dot(yy, x), xr, y)
        size *= 2
    return xr


def _dot2(a, b):
    ah, al = _split2(a)
    bh = b.astype(BF16)
    return jnp.dot(ah, bh, preferred_element_type=F32) + jnp.dot(al, bh, preferred_element_type=F32)


def _dn_kernel(q_ref, k_ref, v_ref, z_ref, ab_ref, cpq_ref, cpk_ref, cpv_ref, cwq_ref, cwk_ref, cwv_ref,
               s0_ref, alog_ref, dtb_ref, nw_ref, o_ref, s_out_ref, s_ref, prev_ref, seq_ref, *, TL, CP, HB):
    hg, i = pl.program_id(1), pl.program_id(2)
    K = DN_CONV - 1
    dk = DN_KDIM

    @pl.when(i == 0)
    def _():
        s_ref[...] = s0_ref[0]
        prev_ref[...] = jnp.zeros_like(prev_ref)
        for n, cp in enumerate((cpq_ref, cpk_ref, cpv_ref)):
            prev_ref[n, 8 - K:8, :] = cp[0]

    def conv(n, x_ref, cw_ref):
        seq_ref[0:8, :] = prev_ref[n]
        seq_ref[8:8 + TL, :] = x_ref[...]
        cw = cw_ref[...]
        acc = seq_ref[8:8 + TL, :] * cw[K:K + 1, :]
        for s in range(1, DN_CONV):
            acc = acc + seq_ref[8 - s:8 - s + TL, :] * cw[K - s:K - s + 1, :]
        prev_ref[n] = seq_ref[TL:TL + 8, :]
        y = _silu(acc)
        if CP > TL:
            y = jnp.concatenate([y, jnp.zeros((CP - TL, y.shape[1]), F32)], axis=0)
        return y

    qc_all = conv(0, q_ref, cwq_ref)
    kc_all = conv(1, k_ref, cwk_ref)
    vc_all = conv(2, v_ref, cwv_ref)
    ab = ab_ref[...]
    lane = lax.broadcasted_iota(I32, ab.shape, 1)
    r, c = _iota2((CP, CP))
    causal = r >= c
    tri = jnp.where(causal, 1.0, 0.0).astype(BF16)
    d = functools.partial(jnp.dot, preferred_element_type=F32)

    heads = range(HB)
    each = lambda f, *ls: [f(*a) for a in zip(*ls)]
    sls = [slice(hh * dk, (hh + 1) * dk) for hh in heads]
    l2n = lambda t: t * lax.rsqrt(jnp.sum(t * t, axis=-1, keepdims=True) + L2_EPS)
    qn = [l2n(qc_all[:, sl]) * (DN_KDIM ** -0.5) for sl in sls]
    kn = [l2n(kc_all[:, sl]) for sl in sls]
    vc = [vc_all[:, sl] for sl in sls]
    pick = lambda col: jnp.sum(jnp.where(lane == col, ab, 0.0), axis=-1, keepdims=True)
    pad0 = (lambda t: jnp.concatenate([t, jnp.zeros((CP - TL, 1), F32)], axis=0)) if CP > TL else (lambda t: t)
    beta = [pad0(_sigmoid(pick(DN_HEADS + hg * HB + hh))) for hh in heads]
    gl = [pad0(-jnp.exp(alog_ref[hg * HB + hh][:, 0:1]) * _softplus(pick(hg * HB + hh) + dtb_ref[hg * HB + hh][:, 0:1]))
          for hh in heads]

    def cumulative(g):
        g1 = jnp.broadcast_to(g, (CP, LANES))
        h1 = g1.astype(BF16)
        r1 = g1 - h1.astype(F32)
        h2 = r1.astype(BF16)
        h3 = (r1 - h2.astype(F32)).astype(BF16)
        return d(tri, h1) + (d(tri, h2) + d(tri, h3))

    gcum_b = [cumulative(g) for g in gl]
    gc = [t[:, 0:1] for t in gcum_b]
    gr = [jnp.transpose(t)[0:1, :] for t in gcum_b]
    decay = each(lambda a, b: jnp.where(causal, jnp.exp(jnp.where(causal, a - b, 0.0)), 0.0), gc, gr)
    kb = each(lambda k, b: k * b, kn, beta)
    lm = each(lambda a, b, dc: jnp.where(r > c, _dot_g(a, b, NT) * dc, 0.0), kb, kn, decay)
    qk = each(lambda a, b, dc: _dot_g(a, b, NT) * dc, qn, kn, decay)
    xr = _unit_lower_inverse_m1(lm, CP)
    eg = [jnp.exp(t) for t in gc]
    solve = lambda x, rhs: rhs + _dot2(x, rhs)
    u = each(lambda x, v, b: solve(x, v * b), xr, vc, beta)
    w = each(lambda x, k, e: solve(x, k * e), xr, kb, eg)
    s = [s_ref[hh] for hh in heads]
    v_new = each(lambda uu, ww, ss: uu - _dot(ww, ss), u, w, s)
    o = each(lambda q, e, ss, a, vn: _dot(q * e, ss) + _dot(a, vn), qn, eg, s, qk, v_new)
    g_last = [t[CP - 1:CP, :] for t in gc]
    s_new = each(lambda ss, gz, k, g, vn: ss * jnp.exp(gz) + _dot_g(k * jnp.exp(gz - g), vn, TN),
                 s, g_last, kn, gc, v_new)
    for hh in heads:
        s_ref[hh] = s_new[hh]
        oh = o[hh][:TL]
        oh = oh * lax.rsqrt(jnp.mean(oh * oh, axis=-1, keepdims=True) + RMS_EPS) * nw_ref[...]
        o_ref[:, sls[hh]] = (oh * _silu(z_ref[:, sls[hh]])).astype(o_ref.dtype)

    @pl.when(i == pl.num_programs(2) - 1)
    def _():
        s_out_ref[0] = s_ref[...]


def _deltanet(proj, ab, conv_prev, state0, conv_w, a_log, dt_bias, norm_w, B, L, *, TL, CP, HB):
    H, dk = DN_HEADS, DN_KDIM
    nl = L // TL
    wb = HB * dk
    cq, ck, cv, cz = COL_QDN // wb, COL_KDN // wb, COL_VDN // wb, COL_ZDN // wb
    row = lambda off: pl.BlockSpec((TL, wb), lambda b, h, i: (b * nl + i, off + h))
    cprev = lambda off: pl.BlockSpec((1, DN_CONV - 1, wb), lambda b, h, i: (b, 0, off + h))
    cwt = lambda off: pl.BlockSpec((DN_CONV, wb), lambda b, h, i: (0, off + h))
    per_head = pl.BlockSpec((H, 1, LANES), lambda b, h, i: (0, 0, 0))
    st = pl.BlockSpec((1, HB, dk, DN_VDIM), lambda b, h, i: (b, h, 0, 0))
    alog_b = jnp.broadcast_to(a_log.astype(F32)[:, None, None], (H, 1, LANES))
    dtb_b = jnp.broadcast_to(dt_bias.astype(F32)[:, None, None], (H, 1, LANES))
    o, s = pl.pallas_call(
        functools.partial(_dn_kernel, TL=TL, CP=CP, HB=HB),
        grid=(B, H // HB, nl),
        in_specs=[row(cq), row(ck), row(cv), row(cz),
                  pl.BlockSpec((TL, LANES), lambda b, h, i: (b * nl + i, 0)),
                  cprev(0), cprev(H // HB), cprev(2 * H // HB), cwt(0), cwt(H // HB), cwt(2 * H // HB),
                  st, per_head, per_head,
                  pl.BlockSpec((1, DN_VDIM), lambda b, h, i: (0, 0))],
        out_specs=[pl.BlockSpec((TL, wb), lambda b, h, i: (b * nl + i, h)), st],
        out_shape=[jax.ShapeDtypeStruct((B * L, H * DN_VDIM), BF16),
                   jax.ShapeDtypeStruct((B, H, dk, DN_VDIM), F32)],
        scratch_shapes=[pltpu.VMEM((HB, dk, DN_VDIM), F32), pltpu.VMEM((3, 8, wb), F32),
                        pltpu.VMEM((TL + 8, wb), F32)],
        compiler_params=_cparams(("arbitrary",) * 3, 48),
        name="deltanet",
    )(proj, proj, proj, proj, ab, conv_prev, conv_prev, conv_prev, conv_w, conv_w, conv_w,
      state0, alog_b, dtb_b, norm_w.reshape(1, DN_VDIM).astype(F32))
    return o, s


def _xattn_kernel(q_ref, mk_ref, mv_ref, o_ref):
    dx = X_HEAD_DIM
    for h in range(X_HEADS):
        sl = slice(h * dx, (h + 1) * dx)
        s = _dot_g(q_ref[:, sl], mk_ref[0][:, sl], NT) * (dx ** -0.5)
        e = jnp.exp(s - jnp.max(s, axis=-1, keepdims=True))
        p = e / jnp.sum(e, axis=-1, keepdims=True)
        o_ref[:, sl] = _dot(p, mv_ref[0][:, sl]).astype(o_ref.dtype)


def _cross_attn(proj, mem_k, mem_v, B, L, *, tm=512):
    tm = min(tm, L)
    nl = L // tm
    mem_blk = pl.BlockSpec((1,) + mem_k.shape[1:], lambda b, i: (b, 0, 0))
    return pl.pallas_call(
        _xattn_kernel,
        grid=(B, nl),
        in_specs=[pl.BlockSpec((tm, X_W), lambda b, i: (b * nl + i, COL_QX // X_W)), mem_blk, mem_blk],
        out_specs=pl.BlockSpec((tm, X_W), lambda b, i: (b * nl + i, 0)),
        out_shape=jax.ShapeDtypeStruct((B * L, X_W), BF16),
        compiler_params=_cparams(("arbitrary", "arbitrary"), 32),
        name="cross_attn",
    )(proj, mem_k, mem_v)


def _merge_kernel(osb_ref, odn_ref, ox_ref, wsb_ref, wdn_ref, wx_ref, gsb_ref, gdn_ref, gx_ref, o_ref):
    acc = _sigmoid(gsb_ref[...]) * _dot(osb_ref[...], wsb_ref[...])
    acc = acc + _sigmoid(gdn_ref[...]) * _dot(odn_ref[...], wdn_ref[...])
    acc = acc + _sigmoid(gx_ref[...]) * _dot(ox_ref[...], wx_ref[...])
    o_ref[...] = acc.astype(o_ref.dtype)


def _merge(o_sb, o_dn, o_x, w_sb_o, w_dn_o, w_x_o, proj, D, *, tm=1024, tn=512):
    T = o_sb.shape[0]
    tm = min(tm, T)
    assert T % tm == 0 and D % tn == 0
    g0 = (COL_QX + X_W) // tn
    act = lambda w: pl.BlockSpec((tm, w), lambda i, j: (i, 0))
    wt = lambda k: pl.BlockSpec((k, tn), lambda i, j: (0, j))
    gate = lambda n: pl.BlockSpec((tm, tn), lambda i, j: (i, g0 + n * (D // tn) + j))
    return pl.pallas_call(
        _merge_kernel,
        grid=(T // tm, D // tn),
        in_specs=[act(SB_W), act(DN_HEADS * DN_VDIM), act(X_W), wt(SB_W), wt(DN_HEADS * DN_VDIM), wt(X_W),
                  gate(0), gate(1), gate(2)],
        out_specs=pl.BlockSpec((tm, tn), lambda i, j: (i, j)),
        out_shape=jax.ShapeDtypeStruct((T, D), BF16),
        compiler_params=_cparams(("arbitrary", "arbitrary"), 48),
        name="merge",
    )(o_sb, o_dn, o_x, w_sb_o, w_dn_o, w_x_o, proj, proj, proj)


def _out_router_kernel(x_ref, m_ref, wo_ref, nw_ref, wr_ref, br_ref, *rest, n_main):
    h_ref, hn_ref, ids_ref, gates_ref = rest[-4:]

    @pl.when(pl.program_id(0) >= n_main)
    def _():
        hn_ref[...] = jnp.zeros_like(hn_ref)

    @pl.when(pl.program_id(0) < n_main)
    def _():
        _out_router_body(x_ref, m_ref, wo_ref, nw_ref, wr_ref, br_ref, h_ref, hn_ref, ids_ref, gates_ref)


def _out_router_body(x_ref, m_ref, wo_ref, nw_ref, wr_ref, br_ref, h_ref, hn_ref, ids_ref, gates_ref):
    hres = x_ref[...] + jnp.dot(m_ref[...], wo_ref[...], preferred_element_type=F32)
    h_ref[...] = hres
    hn = hres * lax.rsqrt(jnp.mean(hres * hres, axis=-1, keepdims=True) + RMS_EPS) * nw_ref[...]
    hn_ref[...] = hn
    logits = _dot3(hn, wr_ref[...]) + br_ref[...]
    lane_i = lax.broadcasted_iota(I32, logits.shape, 1)
    lane = lane_i.astype(F32)
    ninf = -jnp.inf
    first = lambda hit: jnp.min(jnp.where(hit, lane, float(LANES)), axis=-1, keepdims=True)
    gl = jnp.where(lane < N_GROUPS, logits, ninf)
    gmax = jnp.max(gl, axis=-1, keepdims=True)
    gidx = first(gl == gmax)
    g_p = 1.0 / jnp.sum(jnp.where(lane < N_GROUPS, jnp.exp(gl - gmax), 0.0), axis=-1, keepdims=True)
    lo = N_GROUPS + EXPERTS_PER_GROUP * gidx
    el = jnp.where(lane >= lo, jnp.where(lane < lo + EXPERTS_PER_GROUP, logits, ninf), ninf)
    m1 = jnp.max(el, axis=-1, keepdims=True)
    i1 = first(el == m1)
    el2 = jnp.where(lane == i1, ninf, el)
    m2 = jnp.max(el2, axis=-1, keepdims=True)
    i2 = first(el2 == m2)
    e2 = jnp.exp(m2 - m1)
    gate1 = g_p / (1.0 + e2)
    ids = jnp.where(lane_i == 0, i1 - N_GROUPS, jnp.where(lane_i == 1, i2 - N_GROUPS, 0.0))
    ids_ref[...] = ids.astype(I32)
    gates_ref[...] = jnp.where(lane_i == 0, gate1, jnp.where(lane_i == 1, gate1 * e2, 0.0))


def _out_router(x, merged, w_out_bf, nw, w_router, b_router, hn_all, row0, *, tm=512):
    T, D = x.shape
    tm = min(tm, T)
    assert T % tm == 0 and row0 % tm == 0
    n_main = T // tm
    rowf = pl.BlockSpec((tm, D), lambda i: (jnp.minimum(i, n_main - 1), 0))
    const = lambda s: pl.BlockSpec(s, lambda i: (0, 0))
    narrow = pl.BlockSpec((tm, LANES), lambda i: (jnp.minimum(i, n_main - 1), 0))
    in_specs = [rowf, rowf, const((D, D)), const((1, D)), const((D, LANES)), const((1, LANES))]
    args = [x, merged, w_out_bf, nw.reshape(1, D), w_router, b_router]
    if isinstance(hn_all, int):
        assert row0 == 0
        t_all, aliases = pl.cdiv(hn_all, tm) * tm, {}
    else:
        t_all, aliases = hn_all.shape[0], {len(args): 1}
        in_specs.append(pl.BlockSpec(memory_space=pl.ANY))
        args.append(hn_all)
        assert t_all % tm == 0 and row0 + T <= t_all
    n_steps = n_main if aliases else t_all // tm
    return pl.pallas_call(
        functools.partial(_out_router_kernel, n_main=n_main),
        grid=(n_steps,),
        in_specs=in_specs,
        out_specs=[rowf, pl.BlockSpec((tm, D), lambda i: (row0 // tm + i, 0)), narrow, narrow],
        out_shape=[jax.ShapeDtypeStruct((T, D), F32), jax.ShapeDtypeStruct((t_all, D), F32),
                   jax.ShapeDtypeStruct((T, LANES), I32), jax.ShapeDtypeStruct((T, LANES), F32)],
        input_output_aliases=aliases,
        compiler_params=_cparams(("arbitrary",), 56),
        name="out_router",
    )(*args)


def _moe_gather_kernel(tok_ref, hn_ref, o_ref, buf_ref, sem, *, GB):
    i, n = pl.program_id(0), pl.num_programs(0)
    slot = jnp.bitwise_and(i, 1)

    def start(blk, sl):
        def issue(grp, carry):
            for n in range(ISSUE_UNROLL):
                rr = grp * ISSUE_UNROLL + n
                tok = tok_ref[blk * GB + rr]
                pltpu.make_async_copy(hn_ref.at[pl.ds(tok, 1)], buf_ref.at[sl, pl.ds(rr, 1)],
                                      sem.at[sl]).start(priority=n % 2)
            return carry

        lax.fori_loop(0, GB // ISSUE_UNROLL, issue, 0)

    @pl.when(i == 0)
    def _():
        start(0, 0)

    @pl.when(i + 1 < n)
    def _():
        start(i + 1, 1 - slot)

    pltpu.make_async_copy(hn_ref.at[pl.ds(0, GB)], buf_ref.at[slot], sem.at[slot]).wait()
    o_ref[...] = buf_ref[slot].astype(o_ref.dtype)


def _moe_gather(slot_tok, hn, *, GB):
    P = slot_tok.shape[0]
    D = hn.shape[1]
    return pl.pallas_call(
        functools.partial(_moe_gather_kernel, GB=GB),
        grid_spec=pltpu.PrefetchScalarGridSpec(
            num_scalar_prefetch=1, grid=(P // GB,),
            in_specs=[pl.BlockSpec(memory_space=pl.ANY)],
            out_specs=pl.BlockSpec((GB, D), lambda i, tok: (i, 0)),
            scratch_shapes=[pltpu.VMEM((2, GB) + hn.shape[1:], F32), pltpu.SemaphoreType.DMA((2,))]),
        out_shape=jax.ShapeDtypeStruct((P, D), BF16),
        compiler_params=_cparams(("arbitrary",), 24),
        name="moe_gather",
    )(slot_tok, hn)


def _moe_expert_kernel(ie_ref, ib_ref, ins_ref, x_ref, w1_ref, w3_ref, w2_ref, y_ref, acc_ref, *, SUB):
    w, f, s = pl.program_id(0), pl.program_id(1), pl.program_id(2)
    del ie_ref, ib_ref
    nsub = pl.num_programs(2)
    last_f = f == pl.num_programs(1) - 1
    n = ins_ref[w]

    @pl.when(last_f & (s >= nsub + n) & (n < 0))
    def _():
        y_ref[...] = jnp.zeros_like(y_ref)

    @pl.when((s >= nsub - n) & (n > 0))
    def _():
        x = x_ref[...]
        hid = _silu(_dot(x, w1_ref[0])) * _dot(x, w3_ref[0])
        part = _dot(hid, w2_ref[0])
        rows = pl.ds(pl.multiple_of((s - (nsub - n)) * SUB, SUB), SUB)

        @pl.when(f == 0)
        def _():
            acc_ref[rows, :] = part

        @pl.when(f > 0)
        def _():
            acc_ref[rows, :] += part

        @pl.when(last_f)
        def _():
            y_ref[...] = acc_ref[rows, :]


def _moe_experts(xs, item_e, item_b, item_n, w1, w3, w2, *, SUB, NSUB, tf=512):
    P, D = xs.shape
    E, _, DE = w1.shape
    NI = item_e.shape[0]
    nf = DE // tf

    def sub_blk(w, s, ib, ins):
        cnt = jnp.abs(ins[w])
        return ib[w] + jnp.clip(s - (NSUB - cnt), 0, jnp.maximum(cnt - 1, 0))

    return pl.pallas_call(
        functools.partial(_moe_expert_kernel, SUB=SUB),
        grid_spec=pltpu.PrefetchScalarGridSpec(
            num_scalar_prefetch=3, grid=(NI, nf, NSUB),
            in_specs=[
                pl.BlockSpec((SUB, D), lambda w, f, s, ie, ib, ins: (sub_blk(w, s, ib, ins), 0)),
                pl.BlockSpec((1, D, tf), lambda w, f, s, ie, ib, ins: (ie[w], 0, f)),
                pl.BlockSpec((1, D, tf), lambda w, f, s, ie, ib, ins: (ie[w], 0, f)),
                pl.BlockSpec((1, tf, D), lambda w, f, s, ie, ib, ins: (ie[w], f, 0)),
            ],
            out_specs=pl.BlockSpec(
                (SUB, D), lambda w, f, s, ie, ib, ins: (jnp.where(f == nf - 1, sub_blk(w, s, ib, ins), ib[w]), 0)),
            scratch_shapes=[pltpu.VMEM((NSUB * SUB, D), F32)]),
        out_shape=jax.ShapeDtypeStruct((P, D), F32),
        compiler_params=_cparams(("arbitrary",) * 3, 58),
        name="moe_experts",
    )(item_e, item_b, item_n, xs, w1, w3, w2)


def _moe_combine_kernel(pos_ref, h_ref, g_ref, nw_ref, y_hbm, o_ref, b0_ref, b1_ref, sem, *, tm, tok0):
    base = (tok0 + pl.program_id(0) * tm) * TOP_K

    def issue(grp, carry):
        for n in range(ISSUE_UNROLL):
            rr = grp * ISSUE_UNROLL + n
            for k, buf in enumerate((b0_ref, b1_ref)):
                pltpu.make_async_copy(y_hbm.at[pl.ds(pos_ref[base + TOP_K * rr + k], 1)], buf.at[pl.ds(rr, 1)],
                                      sem).start(priority=k)
        return carry

    lax.fori_loop(0, tm // ISSUE_UNROLL, issue, 0)
    pltpu.make_async_copy(y_hbm.at[pl.ds(0, tm)], b0_ref, sem).wait()
    pltpu.make_async_copy(y_hbm.at[pl.ds(0, tm)], b1_ref, sem).wait()
    g = g_ref[...]
    hf = h_ref[...] + (g[:, 0:1] * b0_ref[...] + g[:, 1:2] * b1_ref[...])
    o_ref[...] = hf * lax.rsqrt(jnp.mean(hf * hf, axis=-1, keepdims=True) + RMS_EPS) * nw_ref[...]


def _moe_combine(pos, h, gates, nw, y, tok0, *, tm=256):
    T, D = h.shape
    tm = min(tm, T)
    return pl.pallas_call(
        functools.partial(_moe_combine_kernel, tm=tm, tok0=tok0),
        grid_spec=pltpu.PrefetchScalarGridSpec(
            num_scalar_prefetch=1, grid=(T // tm,),
            in_specs=[pl.BlockSpec((tm, D), lambda i, p: (i, 0)),
                      pl.BlockSpec((tm, LANES), lambda i, p: (i, 0)),
                      pl.BlockSpec((1, D), lambda i, p: (0, 0)),
                      pl.BlockSpec(memory_space=pl.ANY)],
            out_specs=pl.BlockSpec((tm, D), lambda i, p: (i, 0)),
            scratch_shapes=[pltpu.VMEM((tm,) + y.shape[1:], F32), pltpu.VMEM((tm,) + y.shape[1:], F32),
                            pltpu.SemaphoreType.DMA(())]),
        out_shape=jax.ShapeDtypeStruct((T, D), F32),
        compiler_params=_cparams(("arbitrary",), 32),
        name="moe_combine",
    )(pos, h, gates, nw.reshape(1, D), y)


def _moe_plan(e_ids, *, SUB, NSUB):
    A = e_ids.shape[0]
    E = N_EXPERTS
    P = ((A + E * (SUB - 1) + SUB - 1) // SUB) * SUB
    order = jnp.argsort(e_ids, stable=True).astype(I32)
    counts = jnp.sum((e_ids[:, None] == jnp.arange(E, dtype=I32)[None, :]).astype(I32), axis=0)
    padded = ((counts + SUB - 1) // SUB) * SUB
    start = jnp.cumsum(counts) - counts
    pend = jnp.cumsum(padded)
    pstart = pend - padded
    se = e_ids[order]
    dest_sorted = pstart[se] + jnp.arange(A, dtype=I32) - start[se]
    pos = dest_sorted[jnp.argsort(order).astype(I32)]
    p = jnp.arange(P, dtype=I32)
    pe = jnp.minimum(jnp.sum((p[:, None] >= pend[None, :]).astype(I32), axis=1), E - 1)
    rank = p - pstart[pe]
    src = order[jnp.clip(start[pe] + rank, 0, A - 1)] // TOP_K
    slot_tok = jnp.where(rank < counts[pe], src, 0).astype(I32)
    nblk = padded // SUB
    n_items = (nblk + NSUB - 1) // NSUB
    iend = jnp.cumsum(n_items)
    n_blocks = P // SUB
    NI = (n_blocks + E * (NSUB - 1) + NSUB - 1) // NSUB
    w = jnp.arange(NI, dtype=I32)
    total = iend[-1]
    wc = jnp.minimum(w, total - 1)
    ie = jnp.minimum(jnp.sum((wc[:, None] >= iend[None, :]).astype(I32), axis=1), E - 1)
    k = wc - (iend[ie] - n_items[ie])
    ib = pstart[ie] // SUB + k * NSUB
    ins = jnp.minimum(NSUB, nblk[ie] - k * NSUB)
    valid = w < total
    tail0 = jnp.sum(nblk) + (w - total) * NSUB
    tail_n = jnp.clip(n_blocks - tail0, 0, NSUB)
    return (slot_tok, pos, ie.astype(I32), jnp.where(valid, ib, jnp.minimum(tail0, n_blocks - 1)).astype(I32),
            jnp.where(valid, ins, -tail_n).astype(I32))


def _mixer(x2, B, L, sb_fn, mem_k, mem_v, conv_prev, state0, lw, hn_all, row0, *, TL, CP, HB):
    (norm_mix_w, w_a, w_b, w_ab, dn_conv_w, dn_a_log, dn_dt_bias, dn_norm_w,
     w_sb_o, w_dn_o, w_x_o, w_out_bf, norm_ffn_w, w_router, b_router) = lw
    D = x2.shape[1]
    proj, ab = _proj(x2, norm_mix_w, w_a, COL_AB, w_b, w_ab)
    o_sb = sb_fn(proj)
    o_dn, dn_state = _deltanet(proj, ab, conv_prev, state0, dn_conv_w, dn_a_log, dn_dt_bias, dn_norm_w,
                               B, L, TL=TL, CP=CP, HB=HB)
    o_x = _cross_attn(proj, mem_k, mem_v, B, L)
    merged = _merge(o_sb, o_dn, o_x, w_sb_o, w_dn_o, w_x_o, proj, D)
    h, hn, ids, gates = _out_router(x2, merged, w_out_bf, norm_ffn_w, w_router, b_router, hn_all, row0)
    k_new = proj[:, COL_KSB:COL_KSB + SB_W].reshape(B, L, SB_HEADS, SB_HEAD_DIM)
    v_new = proj[:, COL_VSB:COL_VSB + SB_W].reshape(B, L, SB_HEADS, SB_HEAD_DIM)
    K = DN_CONV - 1
    tail = proj.reshape(B, L, proj.shape[1])[:, L - min(L, K):, COL_QDN:COL_ZDN]
    new_conv = jnp.concatenate([conv_prev, tail], axis=1)[:, -K:]
    return h, hn, ids, gates, k_new, v_new, dn_state, new_conv


def kernel(x_prompt, x_sample, cache_sb_k, cache_sb_v, cache_mem_k, cache_mem_v, state_dn, state_dn_conv,
           page_table, mem_prompt, norm_mix_w, w_in, sb_bias, dn_conv_w, dn_a_log, dn_dt_bias, dn_norm_w,
           mem_norm_w, w_mem_k, w_mem_v, w_sb_o, w_dn_o, w_x_o, w_out, norm_ffn_w, w_router_g, b_router_g,
           w_router_e, b_router_e, w_e1, w_e3, w_e2, norm_final_w):
    BP, S, D = x_prompt.shape
    DB, Q, _ = x_sample.shape
    depth = w_in.shape[0]
    n_mem = mem_prompt.shape[1]
    TP, TS = BP * S, DB * Q
    SUB, NSUB = 368, 3
    hp, hs = x_prompt.reshape(TP, D), x_sample.reshape(TS, D)
    outs = {k: [] for k in ("sb_kp", "sb_vp", "mem_kp", "mem_vp", "dn_sp", "dn_cp", "sb_ks", "sb_vs", "dn_ss", "dn_cs")}
    for l in range(depth):
        w_ab = jnp.pad(w_in[l][:, COL_AB:COL_AB + 2 * DN_HEADS], ((0, 0), (0, LANES - 2 * DN_HEADS)))
        w_a = w_in[l][:, :COL_AB].astype(BF16)
        w_b = w_in[l][:, COL_AB + 2 * DN_HEADS:].astype(BF16)
        n_r = N_GROUPS + N_EXPERTS
        w_router = jnp.pad(jnp.concatenate([w_router_g[l], w_router_e[l]], axis=1), ((0, 0), (0, LANES - n_r)))
        b_router = jnp.pad(jnp.concatenate([b_router_g[l], b_router_e[l]]), (0, LANES - n_r)).reshape(1, LANES)
        lw = (norm_mix_w[l], w_a, w_b, w_ab, dn_conv_w[l], dn_a_log[l], dn_dt_bias[l], dn_norm_w[l],
              w_sb_o[l].astype(BF16), w_dn_o[l].astype(BF16), w_x_o[l].astype(BF16), w_out[l].astype(BF16),
              norm_ffn_w[l], w_router, b_router.astype(F32))
        mkv = _proj(mem_prompt.reshape(BP * n_mem, D), mem_norm_w[l], w_mem_k[l], X_W, w_mem_v[l], tm=512, tn=512)
        mk, mv = mkv[:, :X_W].reshape(BP, n_mem, X_W), mkv[:, X_W:].reshape(BP, n_mem, X_W)
        conv0 = jnp.zeros((BP, DN_CONV - 1, 3 * DN_W), F32)
        s0 = jnp.zeros((BP, DN_HEADS, DN_KDIM, DN_VDIM), F32)
        sbp = functools.partial(_sb_prompt, bias=sb_bias[l].astype(F32), B=BP, S=S)
        hp, hn_all, idp, gp, k_new, v_new, s_new, c_new = _mixer(hp, BP, S, sbp, mk, mv, conv0, s0, lw, TP + TS, 0,
                                                                 TL=256, CP=256, HB=4)
        outs["sb_kp"].append(k_new); outs["sb_vp"].append(v_new)
        outs["mem_kp"].append(mk.reshape(BP, n_mem, X_HEADS, X_HEAD_DIM))
        outs["mem_vp"].append(mv.reshape(BP, n_mem, X_HEADS, X_HEAD_DIM))
        outs["dn_sp"].append(s_new); outs["dn_cp"].append(c_new)
        sbs = functools.partial(_sb_sample, bias=sb_bias[l].astype(F32), cache_k=cache_sb_k, cache_v=cache_sb_v,
                                page_table=page_table, layer=l, DB=DB, Q=Q)
        hs, hn_all, ids_, gs, k_new, v_new, s_new, c_new = _mixer(
            hs, DB, Q, sbs, cache_mem_k[l].reshape(DB, n_mem, X_W), cache_mem_v[l].reshape(DB, n_mem, X_W),
            state_dn_conv[l], state_dn[l], lw, hn_all, TP, TL=Q, CP=LANES, HB=DN_HEADS)
        outs["sb_ks"].append(k_new); outs["sb_vs"].append(v_new)
        outs["dn_ss"].append(s_new); outs["dn_cs"].append(c_new)
        e_ids = jnp.concatenate([idp[:, :TOP_K], ids_[:, :TOP_K]], axis=0).reshape(-1)
        slot_tok, pos, ie, ib, ins = _moe_plan(e_ids, SUB=SUB, NSUB=NSUB)
        xs = _moe_gather(slot_tok, hn_all, GB=SUB)
        y = _moe_experts(xs, ie, ib, ins, w_e1[l], w_e3[l], w_e2[l], SUB=SUB, NSUB=NSUB)
        last = l == depth - 1
        nfw = norm_final_w if last else None
        assert last, "multi-layer stacking needs the un-normalised residual; only the final layer applies norm_final"
        hp = _moe_combine(pos, hp, gp, nfw, y, 0)
        hs = _moe_combine(pos, hs, gs, nfw, y, TP)
    stack = lambda k: jnp.stack(outs[k])
    return (hp.reshape(BP, S, D), hs.reshape(DB, Q, D), stack("sb_kp"), stack("sb_vp"), stack("mem_kp"),
            stack("mem_vp"), stack("dn_sp"), stack("dn_cp"), stack("sb_ks"), stack("sb_vs"), stack("dn_ss"),
            stack("dn_cs"))
```
